```python
import math
import jax
import jax.numpy as jnp
from jax import lax
import numpy as np

D_MODEL = 1024
BATCH = 4
SEQ = 8192
DEPTH = 4

DIFF_HEADS = 4
DIFF_WIDTH = D_MODEL // 2
DIFF_V_DIM = DIFF_WIDTH // DIFF_HEADS
DIFF_HEAD_DIM = DIFF_V_DIM // 2
DIFF_QK = DIFF_HEADS * 2 * DIFF_HEAD_DIM
Q_BLOCK = 128
ROPE_THETA = 10000.0
S5_WIDTH = D_MODEL // 4
S5_GROUP = 16
S5_GROUPS = S5_WIDTH // S5_GROUP
S5_STATE = 64
S5_DT_MIN = 0.001
S5_DT_MAX = 0.1
RET_HEADS = 4
RET_WIDTH = D_MODEL // 4
RET_HEAD_DIM = RET_WIDTH // RET_HEADS
RET_CHUNK = 128
MIX_WIDTH = DIFF_WIDTH + S5_WIDTH + RET_WIDTH
IN_WIDTH = 2 * DIFF_QK + DIFF_WIDTH + S5_WIDTH + 4 * RET_WIDTH
N_EXPERTS = 32
TOP_K = 4
EXPERT_FF = D_MODEL
SWIGLU_ALPHA = 1.702
SWIGLU_LIMIT = 7.0
EXPERT_BLOCK = 256
DEEPNORM_ALPHA = (2 * DEPTH) ** 0.25
DEEPNORM_BETA = (8 * DEPTH) ** -0.25
LN_EPS = 1e-5

kernel_name = 'hybrid_diffattn_s5_retention_moe_encoder'


def _layer_norm(x, g=None, b=None):
    xf = x.astype(jnp.float32)
    mu = jnp.mean(xf, axis=-1, keepdims=True)
    var = jnp.mean(jnp.square(xf - mu), axis=-1, keepdims=True)
    y = (xf - mu) * lax.rsqrt(var + LN_EPS)
    if g is not None:
        y = y * g.astype(jnp.float32) + b.astype(jnp.float32)
    return y.astype(x.dtype)


def _rms_norm(x, g):
    xf = x.astype(jnp.float32)
    y = xf * lax.rsqrt(jnp.mean(jnp.square(xf), axis=-1, keepdims=True) + LN_EPS)
    return (y * g.astype(jnp.float32)).astype(x.dtype)


def _rotary_tables(positions, inv_freq):
    ang = positions.astype(jnp.float32)[..., None] * inv_freq
    return jnp.cos(ang)[:, :, None, :], jnp.sin(ang)[:, :, None, :]


def _rotate(x, cos, sin):
    half = x.shape[-1] // 2
    x1, x2 = x[..., :half], x[..., half:]
    cos = cos.astype(x.dtype)
    sin = sin.astype(x.dtype)
    return jnp.concatenate([x1 * cos - x2 * sin, x2 * cos + x1 * sin], axis=-1)


def _split_projection(proj):
    sizes = (DIFF_QK, DIFF_QK, DIFF_WIDTH, S5_WIDTH, RET_WIDTH, RET_WIDTH, RET_WIDTH, RET_WIDTH)
    cuts = []
    acc = 0
    for s in sizes[:-1]:
        acc += s
        cuts.append(acc)
    return jnp.split(proj, cuts, axis=-1)


def _diff_attention(q, k, v, lam, subln_g, lambda_init):
    Bsz, L, H, _, d = q.shape
    scale = d ** -0.5
    qh = q.transpose(0, 2, 3, 1, 4)
    kh = k.transpose(0, 2, 3, 1, 4)
    vh = v.transpose(0, 2, 1, 3)
    nqb = L // Q_BLOCK
    qb = qh.reshape(Bsz, H, 2, nqb, Q_BLOCK, d).transpose(3, 0, 1, 2, 4, 5)

    def block(q_blk):
        s = jnp.einsum('bhcqd,bhckd->bhcqk', q_blk, kh).astype(jnp.float32) * scale
        p = jax.nn.softmax(s, axis=-1)
        a = (p[:, :, 0] - lam * p[:, :, 1]).astype(vh.dtype)
        return jnp.einsum('bhqk,bhkv->bhqv', a, vh)

    o = lax.map(block, qb)
    o = o.transpose(1, 0, 3, 2, 4).reshape(Bsz, L, H, 2 * d)
    o = _rms_norm(o, subln_g) * (1.0 - lambda_init)
    return o.reshape(Bsz, L, H * 2 * d)


def _s5_mixer(u, lam_re, lam_im, log_step, b_re, b_im, c_re, c_im, d_skip, glu_w, glu_b):
    f32 = jnp.float32
    Bsz, L, W = u.shape
    uf = u.astype(f32).reshape(Bsz, L, S5_GROUPS, S5_GROUP)
    lr = jnp.minimum(lam_re.astype(f32), -1e-4)
    li = lam_im.astype(f32)
    dt = jnp.exp(log_step.astype(f32))[..., None]
    mag = jnp.exp(lr * dt)
    ab_re = mag * jnp.cos(li * dt)
    ab_im = mag * jnp.sin(li * dt)
    den = lr * lr + li * li
    f_re = ((ab_re - 1.0) * lr + ab_im * li) / den
    f_im = (ab_im * lr - (ab_re - 1.0) * li) / den
    bre = b_re.astype(f32)
    bim = b_im.astype(f32)
    bb_re = f_re[..., None] * bre - f_im[..., None] * bim
    bb_im = f_re[..., None] * bim + f_im[..., None] * bre

    def combine(e_i, e_j):
        ai_r, ai_i, bi_r, bi_i = e_i
        aj_r, aj_i, bj_r, bj_i = e_j
        return (aj_r * ai_r - aj_i * ai_i,
                aj_r * ai_i + aj_i * ai_r,
                aj_r * bi_r - aj_i * bi_i + bj_r,
                aj_r * bi_i + aj_i * bi_r + bj_i)

    def scan_dir(dirn, reverse):
        bu_r = jnp.einsum('blgn,gpn->blgp', uf, bb_re[dirn])
        bu_i = jnp.einsum('blgn,gpn->blgp', uf, bb_im[dirn])
        shape = (1, L, S5_GROUPS, S5_STATE)
        a_r = jnp.broadcast_to(ab_re[dirn][None, None], shape)
        a_i = jnp.broadcast_to(ab_im[dirn][None, None], shape)
        _, _, h_r, h_i = lax.associative_scan(combine, (a_r, a_i, bu_r, bu_i), reverse=reverse, axis=1)
        return (jnp.einsum('gnp,blgp->blgn', c_re[dirn].astype(f32), h_r)
                - jnp.einsum('gnp,blgp->blgn', c_im[dirn].astype(f32), h_i))

    y = scan_dir(0, False) + scan_dir(1, True)
    y = y.reshape(Bsz, L, W) + d_skip.astype(f32) * u.astype(f32)
    y = jax.nn.gelu(y).astype(u.dtype)
    z = y @ glu_w + glu_b
    return z[..., :W] * jax.nn.sigmoid(z[..., W:])


def _retention(q, k, v, g, cos, sin):
    f32 = jnp.float32
    Bsz, L, H, dh = q.shape
    C = RET_CHUNK
    n = L // C
    dtp = q.dtype
    q = _rotate(q, cos, sin)
    k = _rotate(k, cos, sin) * (dh ** -0.5)

    def chunks(t):
        return t.reshape(Bsz, n, C, H, dh).transpose(0, 3, 1, 2, 4)

    qc, kc, vc = chunks(q), chunks(k), chunks(v)
    hidx = jnp.arange(H, dtype=f32)
    log_gf = jnp.log1p(-jnp.exp2(-5.0 - hidx))
    log_gb = jnp.log1p(-jnp.exp2(-5.5 - hidx))
    j = jnp.arange(C, dtype=f32)
    rel = j[:, None] - j[None, :]
    dmask = jnp.exp(jnp.where(rel >= 0, log_gf[:, None, None], log_gb[:, None, None]) * jnp.abs(rel))
    s = jnp.einsum('bhncd,bhnjd->bhncj', qc, kc) * dmask[None, :, None].astype(dtp)
    y = jnp.einsum('bhncj,bhnje->bhnce', s, vc)
    wf = jnp.exp(log_gf[:, None] * (C - 1.0 - j)).astype(dtp)
    wb = jnp.exp(log_gb[:, None] * j).astype(dtp)
    kv_f = jnp.einsum('bhnjd,hj,bhnje->nbhde', kc, wf, vc)
    kv_b = jnp.einsum('bhnjd,hj,bhnje->nbhde', kc, wb, vc)
    df = jnp.exp(log_gf * C).astype(dtp)[None, :, None, None]
    db = jnp.exp(log_gb * C).astype(dtp)[None, :, None, None]
    zero = jnp.zeros((Bsz, H, dh, dh), dtp)
    _, r_f = lax.scan(lambda st, kv: (df * st + kv, st), zero, kv_f)
    _, r_b = lax.scan(lambda st, kv: (db * st + kv, st), zero, kv_b, reverse=True)
    inter_f = jnp.einsum('bhncd,nbhde->bhnce', qc, r_f) * jnp.exp(log_gf[:, None] * (j + 1.0)).astype(dtp)[None, :, None, :, None]
    inter_b = jnp.einsum('bhncd,nbhde->bhnce', qc, r_b) * jnp.exp(log_gb[:, None] * (C - j)).astype(dtp)[None, :, None, :, None]
    y = (y + inter_f + inter_b).transpose(0, 2, 3, 1, 4).reshape(Bsz, L, H, dh)
    y = _layer_norm(y)
    return (jax.nn.silu(g) * y).reshape(Bsz, L, H * dh)


def _clamped_swiglu(z):
    z_glu = jnp.minimum(z[..., ::2], SWIGLU_LIMIT)
    z_lin = jnp.clip(z[..., 1::2], -SWIGLU_LIMIT, SWIGLU_LIMIT)
    return z_glu * jax.nn.sigmoid(SWIGLU_ALPHA * z_glu) * (z_lin + 1.0)


def _moe(h, router_w, router_b, w1, b1, w2, b2):
    Bsz, L, D = h.shape
    T = Bsz * L
    hf = h.reshape(T, D)
    logits = (hf @ router_w + router_b).astype(jnp.float32)
    top_val, top_idx = lax.top_k(logits, TOP_K)
    gates = jax.nn.softmax(top_val, axis=-1)
    n_assign = T * TOP_K
    n_blocks = n_assign // EXPERT_BLOCK + N_EXPERTS
    e_flat = top_idx.reshape(-1).astype(jnp.int32)
    tok_flat = jnp.arange(n_assign, dtype=jnp.int32) // TOP_K
    order = jnp.argsort(e_flat)
    e_sorted = e_flat[order]
    counts = jnp.zeros((N_EXPERTS,), jnp.int32).at[e_flat].add(1)
    start = jnp.cumsum(counts) - counts
    padded = (counts + EXPERT_BLOCK - 1) // EXPERT_BLOCK * EXPERT_BLOCK
    pend = jnp.cumsum(padded)
    pstart = pend - padded
    dest = pstart[e_sorted] + jnp.arange(n_assign, dtype=jnp.int32) - start[e_sorted]
    n_slots = n_blocks * EXPERT_BLOCK
    slot_tok = jnp.zeros((n_slots,), jnp.int32).at[dest].set(tok_flat[order])
    slot_gate = jnp.zeros((n_slots,), jnp.float32).at[dest].set(gates.reshape(-1)[order])
    block_e = jnp.minimum(jnp.searchsorted(pend, jnp.arange(n_blocks, dtype=jnp.int32) * EXPERT_BLOCK, side='right'),
                          N_EXPERTS - 1).astype(jnp.int32)

    def expert_block(args):
        tok, e = args
        z = hf[tok] @ w1[e] + b1[e]
        return _clamped_swiglu(z) @ w2[e] + b2[e]

    ys = lax.map(expert_block, (slot_tok.reshape(n_blocks, EXPERT_BLOCK), block_e))
    ys = ys.reshape(n_slots, D) * slot_gate[:, None].astype(h.dtype)
    return jax.ops.segment_sum(ys, slot_tok, num_segments=T).reshape(Bsz, L, D)


def setup_inputs(seed: int = 0) -> dict:
    key = jax.random.key(seed)
    ks = jax.random.split(key, 32)
    f32 = jnp.float32

    def nrm(k, shape, s):
        return jax.random.normal(k, shape, f32) * s

    G, P, N = S5_GROUPS, S5_STATE, S5_GROUP
    E, F, D = N_EXPERTS, EXPERT_FF, D_MODEL
    va0 = 2 * DIFF_QK
    vc0 = 2 * DIFF_QK + DIFF_WIDTH + S5_WIDTH + 2 * RET_WIDTH
    col_scale = jnp.ones((IN_WIDTH,), f32).at[va0:va0 + DIFF_WIDTH].set(DEEPNORM_BETA).at[vc0:vc0 + RET_WIDTH].set(DEEPNORM_BETA)
    log_lo, log_hi = math.log(S5_DT_MIN), math.log(S5_DT_MAX)
    return {
        'x': nrm(ks[0], (BATCH, SEQ, D), 1.0),
        'c': nrm(ks[1], (BATCH, D), 1.0),
        'positions': jnp.tile(jnp.arange(SEQ, dtype=jnp.int32)[None, :], (BATCH, 1)),
        'ada_w': nrm(ks[2], (DEPTH, D, 6 * D), 0.1 * D ** -0.5),
        'ada_b': nrm(ks[3], (DEPTH, 6 * D), 0.01),
        'w_in': nrm(ks[4], (DEPTH, D, IN_WIDTH), D ** -0.5) * col_scale,
        'w_out': nrm(ks[5], (DEPTH, MIX_WIDTH, D), MIX_WIDTH ** -0.5 * DEEPNORM_BETA),
        'diff_lambda_q1': nrm(ks[6], (DEPTH, DIFF_HEAD_DIM), 0.1),
        'diff_lambda_k1': nrm(ks[7], (DEPTH, DIFF_HEAD_DIM), 0.1),
        'diff_lambda_q2': nrm(ks[8], (DEPTH, DIFF_HEAD_DIM), 0.1),
        'diff_lambda_k2': nrm(ks[9], (DEPTH, DIFF_HEAD_DIM), 0.1),
        'diff_subln_g': 1.0 + nrm(ks[10], (DEPTH, DIFF_V_DIM), 0.02),
        's5_lambda_re': -0.5 + nrm(ks[11], (DEPTH, 2, G, P), 0.01),
        's5_lambda_im': math.pi * jnp.arange(P, dtype=f32) + nrm(ks[12], (DEPTH, 2, G, P), 0.01),
        's5_log_step': log_lo + jax.random.uniform(ks[13], (DEPTH, 2, G), f32) * (log_hi - log_lo),
        's5_b_re': nrm(ks[14], (DEPTH, 2, G, P, N), (2 * N) ** -0.5),
        's5_b_im': nrm(ks[15], (DEPTH, 2, G, P, N), (2 * N) ** -0.5),
        's5_c_re': nrm(ks[16], (DEPTH, 2, G, N, P), (2 * P) ** -0.5),
        's5_c_im': nrm(ks[17], (DEPTH, 2, G, N, P), (2 * P) ** -0.5),
        's5_d': nrm(ks[18], (DEPTH, S5_WIDTH), 1.0),
        's5_glu_w': nrm(ks[19], (DEPTH, S5_WIDTH, 2 * S5_WIDTH), S5_WIDTH ** -0.5),
        's5_glu_b': nrm(ks[20], (DEPTH, 2 * S5_WIDTH), 0.01),
        'ln_g': 1.0 + nrm(ks[21], (DEPTH, 2, D), 0.02),
        'ln_b': nrm(ks[22], (DEPTH, 2, D), 0.02),
        'router_w': nrm(ks[23], (DEPTH, D, E), D ** -0.5),
        'router_b': nrm(ks[24], (DEPTH, E), 0.01),
        'exp_w1': nrm(ks[25], (DEPTH, E, D, 2 * F), D ** -0.5),
        'exp_b1': nrm(ks[26], (DEPTH, E, 2 * F), 0.01),
        'exp_w2': nrm(ks[27], (DEPTH, E, F, D), F ** -0.5 * DEEPNORM_BETA),
        'exp_b2': nrm(ks[28], (DEPTH, E, D), 0.01),
    }


def reference(x, c, positions, ada_w, ada_b, w_in, w_out, diff_lambda_q1, diff_lambda_k1, diff_lambda_q2,
              diff_lambda_k2, diff_subln_g, s5_lambda_re, s5_lambda_im, s5_log_step, s5_b_re, s5_b_im, s5_c_re,
              s5_c_im, s5_d, s5_glu_w, s5_glu_b, ln_g, ln_b, router_w, router_b, exp_w1, exp_b1, exp_w2, exp_b2):
    f32 = jnp.float32
    Bsz, L, _ = x.shape
    inv_freq_a = 1.0 / (ROPE_THETA ** (jnp.arange(0, DIFF_HEAD_DIM, 2, dtype=f32) / DIFF_HEAD_DIM))
    cos_a, sin_a = _rotary_tables(positions, inv_freq_a)
    inv_freq_r = 1.0 / (ROPE_THETA ** jnp.linspace(0.0, 1.0, RET_HEAD_DIM // 2, dtype=f32))
    cos_r, sin_r = _rotary_tables(positions, inv_freq_r)
    cond = jax.nn.silu(c)
    rs = (Bsz, L, RET_HEADS, RET_HEAD_DIM)
    for l in range(DEPTH):
        mod = cond @ ada_w[l] + ada_b[l]
        sh1, sc1, g1, sh2, sc2, g2 = jnp.split(mod[:, None, :], 6, axis=-1)
        h = _layer_norm(x) * (1.0 + sc1) + sh1
        qa, ka, va, ub, qr, kr, vr, gr = _split_projection(h @ w_in[l])
        qa = _rotate(qa.reshape(Bsz, L, 2 * DIFF_HEADS, DIFF_HEAD_DIM), cos_a, sin_a).reshape(Bsz, L, DIFF_HEADS, 2, DIFF_HEAD_DIM)
        ka = _rotate(ka.reshape(Bsz, L, 2 * DIFF_HEADS, DIFF_HEAD_DIM), cos_a, sin_a).reshape(Bsz, L, DIFF_HEADS, 2, DIFF_HEAD_DIM)
        va = va.reshape(Bsz, L, DIFF_HEADS, DIFF_V_DIM)
        lam_init = 0.8 - 0.6 * math.exp(-0.3 * l)
        lam = (jnp.exp(jnp.sum(diff_lambda_q1[l].astype(f32) * diff_lambda_k1[l].astype(f32)))
               - jnp.exp(jnp.sum(diff_lambda_q2[l].astype(f32) * diff_lambda_k2[l].astype(f32))) + lam_init)
        ya = _diff_attention(qa, ka, va, lam, diff_subln_g[l], lam_init)
        yb = _s5_mixer(ub, s5_lambda_re[l], s5_lambda_im[l], s5_log_step[l], s5_b_re[l], s5_b_im[l],
                       s5_c_re[l], s5_c_im[l], s5_d[l], s5_glu_w[l], s5_glu_b[l])
        yc = _retention(qr.reshape(rs), kr.reshape(rs), vr.reshape(rs), gr.reshape(rs), cos_r, sin_r)
        mix = jnp.concatenate([ya, yb, yc], axis=-1) @ w_out[l]
        x = _layer_norm(DEEPNORM_ALPHA * x + (1.0 + g1) * mix, ln_g[l, 0], ln_b[l, 0])
        h = _layer_norm(x) * (1.0 + sc2) + sh2
        ff = _moe(h, router_w[l], router_b[l], exp_w1[l], exp_b1[l], exp_w2[l], exp_b2[l])
        x = _layer_norm(DEEPNORM_ALPHA * x + (1.0 + g2) * ff, ln_g[l, 1], ln_b[l, 1])
    return x
```

```python
import functools
import math

import jax
import jax.numpy as jnp
from jax import lax
from jax.experimental import pallas as pl
from jax.experimental.pallas import tpu as pltpu

F32 = jnp.float32
BF16 = jnp.bfloat16
HIGHEST = lax.Precision.HIGHEST

LANES = 128
SUBLANES = 8
VMEM_LIMIT = 56 * 1024 * 1024

D_MODEL = 1024
DIFF_HEADS = 4
DIFF_WIDTH = D_MODEL // 2
DIFF_V_DIM = DIFF_WIDTH // DIFF_HEADS
DIFF_HEAD_DIM = DIFF_V_DIM // 2
DIFF_QK = DIFF_HEADS * 2 * DIFF_HEAD_DIM
ROPE_THETA = 10000.0
S5_WIDTH = D_MODEL // 4
S5_GROUP = 16
S5_GROUPS = S5_WIDTH // S5_GROUP
S5_STATE = 64
S5_NS = S5_GROUPS * S5_STATE
RET_HEADS = 4
RET_WIDTH = D_MODEL // 4
RET_HEAD_DIM = RET_WIDTH // RET_HEADS
RET_CHUNK = 128
IN_WIDTH = 2 * DIFF_QK + DIFF_WIDTH + S5_WIDTH + 4 * RET_WIDTH
N_EXPERTS = 32
TOP_K = 4
EXPERT_FF = D_MODEL
SWIGLU_ALPHA = 1.702
SWIGLU_LIMIT = 7.0
EXPERT_BLOCK = 256
LN_EPS = 1e-5
NEG_BIG = -1e30

OFF_QA = 0
OFF_KA = OFF_QA + DIFF_QK
OFF_VA = OFF_KA + DIFF_QK
OFF_UB = OFF_VA + DIFF_WIDTH
OFF_QR = OFF_UB + S5_WIDTH
OFF_KR = OFF_QR + RET_WIDTH
OFF_VR = OFF_KR + RET_WIDTH
OFF_GR = OFF_VR + RET_WIDTH


def _cparams(*sem):
    return pltpu.CompilerParams(dimension_semantics=sem, vmem_limit_bytes=VMEM_LIMIT)


def _ln(x):
    mu = jnp.mean(x, axis=-1, keepdims=True)
    xc = x - mu
    var = jnp.mean(xc * xc, axis=-1, keepdims=True)
    return xc * lax.rsqrt(var + LN_EPS)


def _ada_kernel(c_ref, w_ref, b_ref, o_ref):
    c = c_ref[...]
    cond = c * jax.nn.sigmoid(c)
    o_ref[0] = jnp.dot(cond, w_ref[0], preferred_element_type=F32, precision=HIGHEST) + b_ref[0]


def _ada_call(c_pad, ada_w, ada_b):
    depth, d, n = ada_w.shape
    tn = n // 6
    return pl.pallas_call(
        _ada_kernel,
        grid=(depth, n // tn),
        in_specs=[
            pl.BlockSpec((SUBLANES, d), lambda l, j: (0, 0)),
            pl.BlockSpec((1, d, tn), lambda l, j: (l, 0, j)),
            pl.BlockSpec((1, 1, tn), lambda l, j: (l, 0, j)),
        ],
        out_specs=pl.BlockSpec((1, SUBLANES, tn), lambda l, j: (l, 0, j)),
        out_shape=jax.ShapeDtypeStruct((depth, SUBLANES, n), F32),
        compiler_params=_cparams("arbitrary", "arbitrary"),
        name="ada",
    )(c_pad, ada_w, ada_b.reshape(depth, 1, n))


def _rope_lanes(z, cos, sin_signed, first_half):
    partner = jnp.where(first_half, pltpu.roll(z, LANES - 32, 1), pltpu.roll(z, 32, 1))
    return z * cos + partner * sin_signed


def _pre_kernel(x_ref, sc_ref, sh_ref, w_ref, ca_ref, sa_ref, cr_ref, sr_ref,
                qa_ref, ka_ref, va_ref, ub_ref, qr_ref, kr_ref, vr_ref, gr_ref):
    tm = x_ref.shape[1]
    h = (_ln(x_ref[0]) * (1.0 + sc_ref[0]) + sh_ref[0]).astype(BF16)
    lane = lax.broadcasted_iota(jnp.int32, (tm, LANES), 1)
    first_half = (lane % 64) < 32

    def proj(off, width):
        return jnp.dot(h, w_ref[:, off:off + width], preferred_element_type=F32)

    def rope_store(z, cos, sin, scale, out_ref):
        for j in range(z.shape[1] // LANES):
            zj = _rope_lanes(z[:, j * LANES:(j + 1) * LANES], cos, sin, first_half)
            out_ref[0, :, j * LANES:(j + 1) * LANES] = (zj * scale).astype(out_ref.dtype)

    ca, sa, cr, sr = ca_ref[0], sa_ref[0], cr_ref[0], sr_ref[0]
    rope_store(proj(OFF_QA, DIFF_QK), ca, sa, DIFF_HEAD_DIM ** -0.5, qa_ref)
    rope_store(proj(OFF_KA, DIFF_QK), ca, sa, 1.0, ka_ref)
    va_ref[0] = proj(OFF_VA, DIFF_WIDTH).astype(va_ref.dtype)
    ub_ref[0] = proj(OFF_UB, S5_WIDTH)
    rope_store(proj(OFF_QR, RET_WIDTH), cr, sr, 1.0, qr_ref)
    rope_store(proj(OFF_KR, RET_WIDTH), cr, sr, RET_HEAD_DIM ** -0.5, kr_ref)
    vr_ref[0] = proj(OFF_VR, RET_WIDTH).astype(vr_ref.dtype)
    gr_ref[0] = proj(OFF_GR, RET_WIDTH).astype(gr_ref.dtype)


def _pre_call(x, sc, sh, w_in_bf, cos_a, sin_a, cos_r, sin_r, tm):
    b, l, d = x.shape
    row = lambda bi, i: (bi, i, 0)
    vec = lambda bi, i: (bi, 0, 0)
    widths = (DIFF_QK, DIFF_QK, DIFF_WIDTH, S5_WIDTH, RET_WIDTH, RET_WIDTH, RET_WIDTH, RET_WIDTH)
    dtypes = (BF16, BF16, BF16, F32, BF16, BF16, BF16, BF16)
    return pl.pallas_call(
        _pre_kernel,
        grid=(b, l // tm),
        in_specs=[
            pl.BlockSpec((1, tm, d), row),
            pl.BlockSpec((1, 1, d), vec),
            pl.BlockSpec((1, 1, d), vec),
            pl.BlockSpec((d, IN_WIDTH), lambda bi, i: (0, 0)),
            pl.BlockSpec((1, tm, LANES), row),
            pl.BlockSpec((1, tm, LANES), row),
            pl.BlockSpec((1, tm, LANES), row),
            pl.BlockSpec((1, tm, LANES), row),
        ],
        out_specs=[pl.BlockSpec((1, tm, w), row) for w in widths],
        out_shape=[jax.ShapeDtypeStruct((b, l, w), dt) for w, dt in zip(widths, dtypes)],
        compiler_params=_cparams("arbitrary", "arbitrary"),
        name="pre",
    )(x, sc, sh, w_in_bf, cos_a, sin_a, cos_r, sin_r)


def _attn_kernel(q_ref, k_ref, v_ref, g_ref, lq1_ref, lk1_ref, lq2_ref, lk2_ref, o_ref, *, tk, lam_init):
    tq = q_ref.shape[1]
    seq = k_ref.shape[1]
    q = q_ref[0]
    lane = lax.broadcasted_iota(jnp.int32, (tq, LANES), 1)
    zero = jnp.zeros_like(q)
    q1 = jnp.where(lane < DIFF_HEAD_DIM, q, zero)
    q2 = jnp.where(lane >= DIFF_HEAD_DIM, q, zero)

    def body(j, carry):
        start = pl.multiple_of(j * tk, tk)
        kc = k_ref[0, pl.ds(start, tk), :]
        vc = v_ref[0, pl.ds(start, tk), :]
        out = []
        for qc, (m, l, acc) in zip((q1, q2), (carry[:3], carry[3:])):
            s = lax.dot_general(qc, kc, (((1,), (1,)), ((), ())), preferred_element_type=F32)
            m_new = jnp.maximum(m, jnp.max(s, axis=1, keepdims=True))
            alpha = jnp.exp(m - m_new)
            p = jnp.exp(s - m_new)
            l_new = alpha * l + jnp.sum(p, axis=1, keepdims=True)
            acc_new = alpha * acc + jnp.dot(p.astype(BF16), vc, preferred_element_type=F32)
            out += [m_new, l_new, acc_new]
        return tuple(out)

    m0 = jnp.full((tq, 1), NEG_BIG, F32)
    l0 = jnp.zeros((tq, 1), F32)
    a0 = jnp.zeros((tq, LANES), F32)
    m1, l1, acc1, m2, l2, acc2 = lax.fori_loop(0, seq // tk, body, (m0, l0, a0, m0, l0, a0))
    lam = (jnp.exp(jnp.sum(lq1_ref[...] * lk1_ref[...], axis=1, keepdims=True))
           - jnp.exp(jnp.sum(lq2_ref[...] * lk2_ref[...], axis=1, keepdims=True)) + lam_init)
    o = acc1 / l1 - lam * (acc2 / l2)
    o = o * lax.rsqrt(jnp.mean(o * o, axis=1, keepdims=True) + LN_EPS)
    o_ref[0] = (o * g_ref[...] * (1.0 - lam_init)).astype(o_ref.dtype)


def _attn_call(qa, ka, va, subln_g, lq1, lk1, lq2, lk2, lam_init, tq, tk):
    b, l, _ = qa.shape
    qmap = lambda bi, h, i: (bi, i, h)
    kmap = lambda bi, h, i: (bi, 0, h)
    cmap = lambda bi, h, i: (0, 0)
    small = pl.BlockSpec((1, DIFF_HEAD_DIM), cmap)
    return pl.pallas_call(
        functools.partial(_attn_kernel, tk=tk, lam_init=lam_init),
        grid=(b, DIFF_HEADS, l // tq),
        in_specs=[
            pl.BlockSpec((1, tq, LANES), qmap),
            pl.BlockSpec((1, l, LANES), kmap),
            pl.BlockSpec((1, l, LANES), kmap),
            pl.BlockSpec((1, DIFF_V_DIM), cmap),
            small, small, small, small,
        ],
        out_specs=pl.BlockSpec((1, tq, LANES), qmap),
        out_shape=jax.ShapeDtypeStruct((b, l, DIFF_WIDTH), BF16),
        compiler_params=_cparams("arbitrary", "arbitrary", "arbitrary"),
        name="attn",
    )(qa, ka, va, subln_g, lq1, lk1, lq2, lk2)


def _s5_kernel(u_ref, bf_ref, bb_ref, c_ref, a_ref, y_ref, xs_ref, st_ref, *, tc, rb):
    n = S5_NS
    rows = tc * SUBLANES

    @pl.when(pl.program_id(0) == 0)
    def _():
        st_ref[...] = jnp.zeros_like(st_ref)

    def in_proj(i, c):
        r0 = pl.multiple_of(i * rb, rb)
        u = u_ref[pl.ds(r0, rb), :]
        fwd = (lax.broadcasted_iota(jnp.int32, u.shape, 0) % SUBLANES) < SUBLANES // 2
        zero = jnp.zeros_like(u)
        xs_ref[pl.ds(r0, rb), :] = (
            jnp.dot(jnp.where(fwd, u, zero), bf_ref[...], preferred_element_type=F32)
            + jnp.dot(jnp.where(fwd, zero, u), bb_ref[...], preferred_element_type=F32))
        return c

    lax.fori_loop(0, rows // rb, in_proj, 0)

    a_re = a_ref[:, :n]
    a_im = a_ref[:, n:]

    def step(t, carry):
        h_re, h_im = carry
        r0 = pl.multiple_of(t * SUBLANES, SUBLANES)
        x_re = xs_ref[pl.ds(r0, SUBLANES), :n]
        x_im = xs_ref[pl.ds(r0, SUBLANES), n:]
        n_re = a_re * h_re - a_im * h_im + x_re
        n_im = a_re * h_im + a_im * h_re + x_im
        xs_ref[pl.ds(r0, SUBLANES), :n] = n_re
        xs_ref[pl.ds(r0, SUBLANES), n:] = n_im
        return n_re, n_im

    h_re, h_im = lax.fori_loop(0, tc, step, (st_ref[:, :n], st_ref[:, n:]), unroll=4)
    st_ref[:, :n] = h_re
    st_ref[:, n:] = h_im

    def out_proj(i, c):
        r0 = pl.multiple_of(i * rb, rb)
        hs = xs_ref[pl.ds(r0, rb), :].astype(BF16)
        yy = jnp.dot(hs, c_ref[...], preferred_element_type=F32)
        fwd = (lax.broadcasted_iota(jnp.int32, (rb, S5_WIDTH), 0) % SUBLANES) < SUBLANES // 2
        y_ref[pl.ds(r0, rb), :] = jnp.where(fwd, yy[:, :S5_WIDTH], yy[:, S5_WIDTH:])
        return c

    lax.fori_loop(0, rows // rb, out_proj, 0)


def _s5_call(u_int, bmat_f, bmat_b, cmat, a_rows, tc):
    rows_total = u_int.shape[0]
    rows = tc * SUBLANES
    rb = min(rows, 256)
    full = lambda i: (0, 0)
    return pl.pallas_call(
        functools.partial(_s5_kernel, tc=tc, rb=rb),
        grid=(rows_total // rows,),
        in_specs=[
            pl.BlockSpec((rows, S5_WIDTH), lambda i: (i, 0)),
            pl.BlockSpec(bmat_f.shape, full),
            pl.BlockSpec(bmat_b.shape, full),
            pl.BlockSpec(cmat.shape, full),
            pl.BlockSpec(a_rows.shape, full),
        ],
        out_specs=pl.BlockSpec((rows, S5_WIDTH), lambda i: (i, 0)),
        out_shape=jax.ShapeDtypeStruct((rows_total, S5_WIDTH), F32),
        scratch_shapes=[pltpu.VMEM((rows, 2 * S5_NS), F32), pltpu.VMEM((SUBLANES, 2 * S5_NS), F32)],
        compiler_params=_cparams("arbitrary"),
        name="s5",
    )(u_int, bmat_f, bmat_b, cmat, a_rows)


def _s5_params(lam_re, lam_im, log_step, b_re, b_im, c_re, c_im, batch):
    lr = jnp.minimum(lam_re.astype(F32), -1e-4)
    li = lam_im.astype(F32)
    dt = jnp.exp(log_step.astype(F32))[..., None]
    mag = jnp.exp(lr * dt)
    ab_re = mag * jnp.cos(li * dt)
    ab_im = mag * jnp.sin(li * dt)
    den = lr * lr + li * li
    f_re = ((ab_re - 1.0) * lr + ab_im * li) / den
    f_im = (ab_im * lr - (ab_re - 1.0) * li) / den
    bre = b_re.astype(F32)
    bim = b_im.astype(F32)
    bb_re = f_re[..., None] * bre - f_im[..., None] * bim
    bb_im = f_re[..., None] * bim + f_im[..., None] * bre
    eye = jnp.eye(S5_GROUPS, dtype=F32)

    def in_mat(m):
        return jnp.einsum('gpn,gh->gnhp', m, eye).reshape(S5_WIDTH, S5_NS)

    def out_mat(m):
        return jnp.einsum('gnp,gh->gphn', m, eye).reshape(S5_NS, S5_WIDTH)

    bmats = [jnp.concatenate([in_mat(bb_re[d]), in_mat(bb_im[d])], axis=1).astype(BF16) for d in range(2)]
    cmat = jnp.concatenate(
        [jnp.concatenate([out_mat(c_re[d].astype(F32)), -out_mat(c_im[d].astype(F32))], axis=0) for d in range(2)],
        axis=1).astype(BF16)
    a_dir = jnp.concatenate([ab_re.reshape(2, S5_NS), ab_im.reshape(2, S5_NS)], axis=1)
    a_rows = jnp.repeat(a_dir, batch, axis=0)
    return bmats[0], bmats[1], cmat, a_rows


def _ret_kernel(q_ref, k_ref, v_ref, g_ref, dm_ref, wf_ref, wb_ref, gf_ref, gb_ref, dec_ref,
                o_ref, y_ref, *, chunk):
    seq = q_ref.shape[1]
    n = seq // chunk
    lane = lax.broadcasted_iota(jnp.int32, (chunk, LANES), 1)
    head_lanes = [lane < RET_HEAD_DIM, lane >= RET_HEAD_DIM]
    nt = (((1,), (1,)), ((), ()))
    tn = (((0,), (0,)), ((), ()))

    def load(ref, c0):
        return ref[0, pl.ds(c0, chunk), :]

    def fwd(i, states):
        c0 = pl.multiple_of(i * chunk, chunk)
        qc, kc, vc = load(q_ref, c0), load(k_ref, c0), load(v_ref, c0)
        y = jnp.zeros((chunk, LANES), F32)
        new_states = []
        for h in range(2):
            qm = jnp.where(head_lanes[h], qc, jnp.zeros_like(qc))
            s = lax.dot_general(qm, kc, nt, preferred_element_type=F32) * dm_ref[h]
            yh = jnp.dot(s.astype(BF16), vc, preferred_element_type=F32)
            yh = yh + jnp.dot(qm, states[h].astype(BF16), preferred_element_type=F32) * gf_ref[h]
            kw = (kc.astype(F32) * wf_ref[h]).astype(BF16)
            new_states.append(dec_ref[h, 0:1, :] * states[h]
                              + lax.dot_general(kw, vc, tn, preferred_element_type=F32))
            y = jnp.where(head_lanes[h], yh, y)
        y_ref[pl.ds(c0, chunk), :] = y
        return tuple(new_states)

    z = jnp.zeros((LANES, LANES), F32)
    lax.fori_loop(0, n, fwd, (z, z))

    def bwd(ii, states):
        i = n - 1 - ii
        c0 = pl.multiple_of(i * chunk, chunk)
        qc, kc, vc = load(q_ref, c0), load(k_ref, c0), load(v_ref, c0)
        y = y_ref[pl.ds(c0, chunk), :]
        new_states = []
        for h in range(2):
            qm = jnp.where(head_lanes[h], qc, jnp.zeros_like(qc))
            yh = jnp.dot(qm, states[h].astype(BF16), preferred_element_type=F32) * gb_ref[h]
            kw = (kc.astype(F32) * wb_ref[h]).astype(BF16)
            new_states.append(dec_ref[h, 1:2, :] * states[h]
                              + lax.dot_general(kw, vc, tn, preferred_element_type=F32))
            y = y + jnp.where(head_lanes[h], yh, 0.0)
        inv = 1.0 / RET_HEAD_DIM
        mu = [jnp.sum(jnp.where(m, y, 0.0), axis=1, keepdims=True) * inv for m in head_lanes]
        yc = y - jnp.where(head_lanes[0], mu[0], mu[1])
        var = [jnp.sum(jnp.where(m, yc * yc, 0.0), axis=1, keepdims=True) * inv for m in head_lanes]
        yn = yc * lax.rsqrt(jnp.where(head_lanes[0], var[0], var[1]) + LN_EPS)
        g = load(g_ref, c0).astype(F32)
        o_ref[0, pl.ds(c0, chunk), :] = (g * jax.nn.sigmoid(g) * yn).astype(o_ref.dtype)
        return tuple(new_states)

    lax.fori_loop(0, n, bwd, (z, z))


def _ret_tables(chunk):
    hidx = jnp.arange(RET_HEADS, dtype=F32)
    log_gf = jnp.log1p(-jnp.exp2(-5.0 - hidx))
    log_gb = jnp.log1p(-jnp.exp2(-5.5 - hidx))
    j = jnp.arange(chunk, dtype=F32)
    rel = j[:, None] - j[None, :]
    dmask = jnp.exp(jnp.where(rel >= 0, log_gf[:, None, None], log_gb[:, None, None]) * jnp.abs(rel))
    rows = lambda t: jnp.broadcast_to(t[:, :, None], (RET_HEADS, chunk, LANES))
    wf = rows(jnp.exp(log_gf[:, None] * (chunk - 1.0 - j)))
    wb = rows(jnp.exp(log_gb[:, None] * j))
    gf = rows(jnp.exp(log_gf[:, None] * (j + 1.0)))
    gb = rows(jnp.exp(log_gb[:, None] * (chunk - j)))
    dec = jnp.stack([jnp.exp(log_gf * chunk), jnp.exp(log_gb * chunk)], axis=1)
    dec = jnp.broadcast_to(jnp.pad(dec, ((0, 0), (0, SUBLANES - 2)))[:, :, None], (RET_HEADS, SUBLANES, LANES))
    return dmask, wf, wb, gf, gb, dec


def _ret_call(qr, kr, vr, gr, tables, chunk):
    b, l, _ = qr.shape
    dmask, wf, wb, gf, gb, dec = tables
    seq = lambda bi, p: (bi, 0, p)
    pair = lambda bi, p: (p, 0, 0)
    seq_spec = pl.BlockSpec((1, l, LANES), seq)
    row_spec = pl.BlockSpec((2, chunk, LANES), pair)
    return pl.pallas_call(
        functools.partial(_ret_kernel, chunk=chunk),
        grid=(b, RET_HEADS // 2),
        in_specs=[seq_spec, seq_spec, seq_spec, seq_spec,
                  pl.BlockSpec((2, chunk, chunk), pair),
                  row_spec, row_spec, row_spec, row_spec,
                  pl.BlockSpec((2, SUBLANES, LANES), pair)],
        out_specs=seq_spec,
        out_shape=jax.ShapeDtypeStruct((b, l, RET_WIDTH), BF16),
        scratch_shapes=[pltpu.VMEM((l, LANES), F32)],
        compiler_params=_cparams("arbitrary", "arbitrary"),
        name="ret",
    )(qr, kr, vr, gr, dmask, wf, wb, gf, gb, dec)


def _post_kernel(ya_ref, yf_ref, yb_ref, ub_ref, yc_ref, x_ref, dsk_ref, gw_ref, gb_ref,
                 woa_ref, wob_ref, woc_ref, g1_ref, lng_ref, lnb_ref, sc2_ref, sh2_ref, rwt_ref, rb_ref,
                 x1_ref, h2_ref, idx_ref, gate_ref, *, alpha):
    y = yf_ref[0] + yb_ref[0] + dsk_ref[...] * ub_ref[0]
    y = jax.nn.gelu(y).astype(BF16)
    z = jnp.dot(y, gw_ref[...], preferred_element_type=F32) + gb_ref[...]
    yb = (z[:, :S5_WIDTH] * jax.nn.sigmoid(z[:, S5_WIDTH:])).astype(BF16)
    mix = (jnp.dot(ya_ref[0], woa_ref[...], preferred_element_type=F32)
           + jnp.dot(yb, wob_ref[...], preferred_element_type=F32)
           + jnp.dot(yc_ref[0], woc_ref[...], preferred_element_type=F32))
    x1 = _ln(alpha * x_ref[0] + (1.0 + g1_ref[0]) * mix) * lng_ref[...] + lnb_ref[...]
    x1_ref[0] = x1
    h2 = _ln(x1) * (1.0 + sc2_ref[0]) + sh2_ref[0]
    h2_ref[0] = h2
    logits = lax.dot_general(rwt_ref[...], h2, (((1,), (1,)), ((), ())),
                             preferred_element_type=F32, precision=HIGHEST) + rb_ref[...]
    eidx = lax.broadcasted_iota(jnp.int32, logits.shape, 0)
    vals, idxs = [], []
    cur = logits
    for _ in range(TOP_K):
        m = jnp.max(cur, axis=0, keepdims=True)
        i = jnp.min(jnp.where(cur == m, eidx, N_EXPERTS), axis=0, keepdims=True)
        vals.append(m)
        idxs.append(i)
        cur = jnp.where(eidx == i, -jnp.inf, cur)
    tv = jnp.concatenate(vals, axis=0)
    e = jnp.exp(tv - vals[0])
    gate_ref[...] = e / jnp.sum(e, axis=0, keepdims=True)
    idx_ref[...] = jnp.concatenate(idxs, axis=0)


def _post_call(ya, yf, yb, ub, yc, x, dsk, gw, gb, woa, wob, woc, g1, lng, lnb, sc2, sh2, rwt, rb, alpha, tm):
    b, l, d = x.shape
    row = lambda bi, i: (bi, i, 0)
    vec = lambda bi, i: (bi, 0, 0)
    full2 = lambda bi, i: (0, 0)
    nl = l // tm
    tok = lambda bi, i: (0, bi * nl + i)

    def fs(a):
        return pl.BlockSpec(a.shape, full2)

    return pl.pallas_call(
        functools.partial(_post_kernel, alpha=alpha),
        grid=(b, nl),
        in_specs=[
            pl.BlockSpec((1, tm, DIFF_WIDTH), row),
            pl.BlockSpec((1, tm, S5_WIDTH), row),
            pl.BlockSpec((1, tm, S5_WIDTH), row),
            pl.BlockSpec((1, tm, S5_WIDTH), row),
            pl.BlockSpec((1, tm, RET_WIDTH), row),
            pl.BlockSpec((1, tm, d), row),
            fs(dsk), fs(gw), fs(gb), fs(woa), fs(wob), fs(woc),
            pl.BlockSpec((1, 1, d), vec),
            fs(lng), fs(lnb),
            pl.BlockSpec((1, 1, d), vec),
            pl.BlockSpec((1, 1, d), vec),
            fs(rwt), fs(rb),
        ],
        out_specs=[
            pl.BlockSpec((1, tm, d), row),
            pl.BlockSpec((1, tm, d), row),
            pl.BlockSpec((TOP_K, tm), tok),
            pl.BlockSpec((TOP_K, tm), tok),
        ],
        out_shape=[
            jax.ShapeDtypeStruct((b, l, d), F32),
            jax.ShapeDtypeStruct((b, l, d), F32),
            jax.ShapeDtypeStruct((TOP_K, b * l), jnp.int32),
            jax.ShapeDtypeStruct((TOP_K, b * l), F32),
        ],
        compiler_params=_cparams("arbitrary", "arbitrary"),
        name="post",
    )(ya, yf, yb, ub, yc, x, dsk, gw, gb, woa, wob, woc, g1, lng, lnb, sc2, sh2, rwt, rb)


def _row_gather(src_hbm, dst, sem, src_row, dst_row):
    return pltpu.make_async_copy(src_hbm.at[pl.ds(src_row, 1), :], dst.at[pl.ds(dst_row, 1), :], sem)


def _expert_kernel(be_ref, nu_ref, tok_ref, h_hbm, w1_ref, b1_ref, w2_ref, b2_ref, o_ref, buf, sem):
    rows = buf.shape[0]
    ff = w2_ref.shape[1]

    @pl.when(pl.program_id(0) < nu_ref[0])
    def _():
        def issue(r, c):
            _row_gather(h_hbm, buf, sem, tok_ref[0, 0, r], r).start()
            return c

        lax.fori_loop(0, rows, issue, 0)

        def drain(r, c):
            _row_gather(h_hbm, buf, sem, 0, r).wait()
            return c

        lax.fori_loop(0, rows, drain, 0)
        hb = buf[...].astype(BF16)
        z = jnp.dot(hb, w1_ref[0], preferred_element_type=F32) + b1_ref[0]
        z_glu = jnp.minimum(z[:, :ff], SWIGLU_LIMIT)
        z_lin = jnp.clip(z[:, ff:], -SWIGLU_LIMIT, SWIGLU_LIMIT)
        act = (z_glu * jax.nn.sigmoid(SWIGLU_ALPHA * z_glu) * (z_lin + 1.0)).astype(BF16)
        o_ref[...] = jnp.dot(act, w2_ref[0], preferred_element_type=F32) + b2_ref[0]

    @pl.when(pl.program_id(0) >= nu_ref[0])
    def _():
        o_ref[...] = jnp.zeros_like(o_ref)


def _expert_call(block_e, n_used, slot_tok, h2, w1, b1, w2, b2):
    n_blocks = block_e.shape[0]
    t, d = h2.shape
    ff = w2.shape[1]
    eb = EXPERT_BLOCK
    grid_spec = pltpu.PrefetchScalarGridSpec(
        num_scalar_prefetch=2,
        grid=(n_blocks,),
        in_specs=[
            pl.BlockSpec((1, 1, eb), lambda i, be, nu: (i, 0, 0), memory_space=pltpu.SMEM),
            pl.BlockSpec(memory_space=pl.ANY),
            pl.BlockSpec((1, d, 2 * ff), lambda i, be, nu: (be[i], 0, 0)),
            pl.BlockSpec((1, 1, 2 * ff), lambda i, be, nu: (be[i], 0, 0)),
            pl.BlockSpec((1, ff, d), lambda i, be, nu: (be[i], 0, 0)),
            pl.BlockSpec((1, 1, d), lambda i, be, nu: (be[i], 0, 0)),
        ],
        out_specs=pl.BlockSpec((eb, d), lambda i, be, nu: (i, 0)),
        scratch_shapes=[pltpu.VMEM((eb, d), F32), pltpu.SemaphoreType.DMA(())],
    )
    return pl.pallas_call(
        _expert_kernel,
        grid_spec=grid_spec,
        out_shape=jax.ShapeDtypeStruct((n_blocks * eb, d), F32),
        compiler_params=_cparams("arbitrary"),
        name="expert",
    )(block_e, n_used, slot_tok.reshape(n_blocks, 1, eb), h2, w1, b1, w2, b2)


def _combine_kernel(dest_ref, ys_hbm, gate_ref, x1_ref, g2_ref, lng_ref, lnb_ref, o_ref, buf, sem, *, alpha):
    tm = x1_ref.shape[1]

    def issue(r, c):
        _row_gather(ys_hbm, buf.at[r % TOP_K], sem, dest_ref[0, 0, r], r // TOP_K).start()
        return c

    lax.fori_loop(0, tm * TOP_K, issue, 0)

    def drain(r, c):
        _row_gather(ys_hbm, buf.at[r % TOP_K], sem, 0, r // TOP_K).wait()
        return c

    lax.fori_loop(0, tm * TOP_K, drain, 0)
    gates = gate_ref[...]
    ffo = buf[0] * gates[:, 0:1]
    for k in range(1, TOP_K):
        ffo = ffo + buf[k] * gates[:, k:k + 1]
    o_ref[0] = _ln(alpha * x1_ref[0] + (1.0 + g2_ref[0]) * ffo) * lng_ref[...] + lnb_ref[...]


def _combine_call(dest, ys, gates_t, x1, g2, lng, lnb, alpha, tm):
    b, l, d = x1.shape
    nl = l // tm
    n_tiles = b * nl
    row = lambda bi, i: (bi, i, 0)
    vec = lambda bi, i: (bi, 0, 0)
    full2 = lambda bi, i: (0, 0)
    return pl.pallas_call(
        functools.partial(_combine_kernel, alpha=alpha),
        grid=(b, nl),
        in_specs=[
            pl.BlockSpec((1, 1, tm * TOP_K), lambda bi, i: (bi * nl + i, 0, 0), memory_space=pltpu.SMEM),
            pl.BlockSpec(memory_space=pl.ANY),
            pl.BlockSpec((tm, TOP_K), lambda bi, i: (bi * nl + i, 0)),
            pl.BlockSpec((1, tm, d), row),
            pl.BlockSpec((1, 1, d), vec),
            pl.BlockSpec(lng.shape, full2),
            pl.BlockSpec(lnb.shape, full2),
        ],
        out_specs=pl.BlockSpec((1, tm, d), row),
        out_shape=jax.ShapeDtypeStruct((b, l, d), F32),
        scratch_shapes=[pltpu.VMEM((TOP_K, tm, d), F32), pltpu.SemaphoreType.DMA(())],
        compiler_params=_cparams("arbitrary", "arbitrary"),
        name="combine",
    )(dest.reshape(n_tiles, 1, tm * TOP_K), ys, gates_t, x1, g2, lng, lnb)


def _route(top_idx, n_tok):
    n_assign = n_tok * TOP_K
    n_blocks = n_assign // EXPERT_BLOCK + N_EXPERTS
    e_flat = top_idx.T.reshape(-1)
    order = jnp.argsort(e_flat)
    e_sorted = e_flat[order]
    counts = jnp.sum((e_flat[:, None] == jnp.arange(N_EXPERTS, dtype=jnp.int32)[None, :]).astype(jnp.int32), axis=0)
    start = jnp.cumsum(counts) - counts
    padded = (counts + EXPERT_BLOCK - 1) // EXPERT_BLOCK * EXPERT_BLOCK
    pend = jnp.cumsum(padded)
    pstart = pend - padded
    dest_sorted = pstart[e_sorted] + jnp.arange(n_assign, dtype=jnp.int32) - start[e_sorted]
    _, dest = lax.sort((order.astype(jnp.int32), dest_sorted.astype(jnp.int32)), num_keys=1)
    block_start = jnp.arange(n_blocks, dtype=jnp.int32) * EXPERT_BLOCK
    block_e = jnp.minimum(jnp.searchsorted(pend, block_start, side='right'), N_EXPERTS - 1).astype(jnp.int32)
    slot = jnp.arange(n_blocks * EXPERT_BLOCK, dtype=jnp.int32)
    slot_e = jnp.repeat(block_e, EXPERT_BLOCK)
    within = slot - pstart[slot_e]
    valid = (within < counts[slot_e]) & (slot < pend[-1])
    src = jnp.clip(start[slot_e] + within, 0, n_assign - 1)
    slot_tok = jnp.where(valid, (order[src] // TOP_K).astype(jnp.int32), 0)
    n_used = (pend[-1] // EXPERT_BLOCK).astype(jnp.int32).reshape(1)
    return block_e, n_used, slot_tok, dest


def _pick(n, pref):
    return pref if n % pref == 0 else n


def kernel(x, c, positions, ada_w, ada_b, w_in, w_out, diff_lambda_q1, diff_lambda_k1, diff_lambda_q2,
           diff_lambda_k2, diff_subln_g, s5_lambda_re, s5_lambda_im, s5_log_step, s5_b_re, s5_b_im, s5_c_re,
           s5_c_im, s5_d, s5_glu_w, s5_glu_b, ln_g, ln_b, router_w, router_b, exp_w1, exp_b1, exp_w2, exp_b2):
    bsz, seq, d = x.shape
    depth = ada_w.shape[0]
    n_tok = bsz * seq
    alpha = (2 * depth) ** 0.25

    def tables(inv_freq):
        ang = positions.astype(F32)[..., None] * inv_freq
        cs, sn = jnp.cos(ang), jnp.sin(ang)
        return jnp.concatenate([cs, cs, cs, cs], axis=-1), jnp.concatenate([-sn, sn, -sn, sn], axis=-1)

    cos_a, sin_a = tables(1.0 / (ROPE_THETA ** (jnp.arange(0, DIFF_HEAD_DIM, 2, dtype=F32) / DIFF_HEAD_DIM)))
    cos_r, sin_r = tables(1.0 / (ROPE_THETA ** jnp.linspace(0.0, 1.0, RET_HEAD_DIM // 2, dtype=F32)))
    ret_tables = _ret_tables(RET_CHUNK)

    c_pad = jnp.pad(c.astype(F32), ((0, SUBLANES - bsz), (0, 0)))
    mod = _ada_call(c_pad, ada_w, ada_b)[:, :bsz]

    tm_pre = _pick(seq, 512)
    tq = _pick(seq, 256)
    tk = _pick(seq, 512)
    tc = _pick(seq, 128)
    tm_post = _pick(seq, 256)
    tm_comb = _pick(seq, 128)

    for l in range(depth):
        sh1, sc1, g1, sh2, sc2, g2 = [mod[l, :, None, i * d:(i + 1) * d] for i in range(6)]
        qa, ka, va, ub, qr, kr, vr, gr = _pre_call(
            x, sc1, sh1, w_in[l].astype(BF16), cos_a, sin_a, cos_r, sin_r, tm_pre)

        lam_init = 0.8 - 0.6 * math.exp(-0.3 * l)
        ya = _attn_call(qa, ka, va, diff_subln_g[l][None, :].astype(F32),
                        diff_lambda_q1[l][None, :], diff_lambda_k1[l][None, :],
                        diff_lambda_q2[l][None, :], diff_lambda_k2[l][None, :], lam_init, tq, tk)

        u_seq = jnp.concatenate([ub, ub[:, ::-1]], axis=0)
        u_int = u_seq.transpose(1, 0, 2).reshape(seq * 2 * bsz, S5_WIDTH).astype(BF16)
        bmat_f, bmat_b, cmat, a_rows = _s5_params(
            s5_lambda_re[l], s5_lambda_im[l], s5_log_step[l], s5_b_re[l], s5_b_im[l], s5_c_re[l], s5_c_im[l], bsz)
        y_int = _s5_call(u_int, bmat_f, bmat_b, cmat, a_rows, tc).reshape(seq, 2 * bsz, S5_WIDTH)
        y_fwd = y_int[:, :bsz].transpose(1, 0, 2)
        y_bwd = y_int[::-1, bsz:].transpose(1, 0, 2)

        yc = _ret_call(qr, kr, vr, gr, ret_tables, RET_CHUNK)

        wo = w_out[l].astype(BF16)
        x1, h2, top_idx, gates = _post_call(
            ya, y_fwd, y_bwd, ub, yc, x,
            s5_d[l][None, :].astype(F32), s5_glu_w[l].astype(BF16), s5_glu_b[l][None, :].astype(F32),
            wo[:DIFF_WIDTH], wo[DIFF_WIDTH:DIFF_WIDTH + S5_WIDTH], wo[DIFF_WIDTH + S5_WIDTH:],
            g1, ln_g[l, 0][None, :], ln_b[l, 0][None, :], sc2, sh2,
            router_w[l].T.astype(F32), router_b[l][:, None].astype(F32), alpha, tm_post)

        block_e, n_used, slot_tok, dest = _route(top_idx, n_tok)
        w1 = jnp.concatenate([exp_w1[l][:, :, 0::2], exp_w1[l][:, :, 1::2]], axis=-1).astype(BF16)
        b1 = jnp.concatenate([exp_b1[l][:, 0::2], exp_b1[l][:, 1::2]], axis=-1)[:, None, :].astype(F32)
        ys = _expert_call(block_e, n_used, slot_tok, h2.reshape(n_tok, d), w1, b1,
                          exp_w2[l].astype(BF16), exp_b2[l][:, None, :].astype(F32))
        x = _combine_call(dest, ys, gates.T, x1, g2, ln_g[l, 1][None, :], ln_b[l, 1][None, :], alpha, tm_comb)
    return x
```

```python
import functools
import math

import jax
import jax.numpy as jnp
from jax import lax
from jax.experimental import pallas as pl
from jax.experimental.pallas import tpu as pltpu

F32 = jnp.float32
BF16 = jnp.bfloat16
HIGHEST = lax.Precision.HIGHEST

LANES = 128
SUBLANES = 8
VMEM_LIMIT = 56 * 1024 * 1024

D_MODEL = 1024
DIFF_HEADS = 4
DIFF_WIDTH = D_MODEL // 2
DIFF_V_DIM = DIFF_WIDTH // DIFF_HEADS
DIFF_HEAD_DIM = DIFF_V_DIM // 2
DIFF_QK = DIFF_HEADS * 2 * DIFF_HEAD_DIM
ROPE_THETA = 10000.0
S5_WIDTH = D_MODEL // 4
S5_GROUP = 16
S5_GROUPS = S5_WIDTH // S5_GROUP
S5_STATE = 64
S5_NS = S5_GROUPS * S5_STATE
RET_HEADS = 4
RET_WIDTH = D_MODEL // 4
RET_HEAD_DIM = RET_WIDTH // RET_HEADS
RET_CHUNK = 128
IN_WIDTH = 2 * DIFF_QK + DIFF_WIDTH + S5_WIDTH + 4 * RET_WIDTH
N_EXPERTS = 32
TOP_K = 4
EXPERT_FF = D_MODEL
SWIGLU_ALPHA = 1.702
SWIGLU_LIMIT = 7.0
EXPERT_BLOCK = 256
LN_EPS = 1e-5
NEG_BIG = -1e30

OFF_QA = 0
OFF_KA = OFF_QA + DIFF_QK
OFF_VA = OFF_KA + DIFF_QK
OFF_UB = OFF_VA + DIFF_WIDTH
OFF_QR = OFF_UB + S5_WIDTH
OFF_KR = OFF_QR + RET_WIDTH
OFF_VR = OFF_KR + RET_WIDTH
OFF_GR = OFF_VR + RET_WIDTH


def _cparams(*sem):
    return pltpu.CompilerParams(dimension_semantics=sem, vmem_limit_bytes=VMEM_LIMIT)


def _ln(x):
    mu = jnp.mean(x, axis=-1, keepdims=True)
    xc = x - mu
    var = jnp.mean(xc * xc, axis=-1, keepdims=True)
    return xc * lax.rsqrt(var + LN_EPS)


def _ada_kernel(c_ref, w_ref, b_ref, o_ref):
    c = c_ref[...]
    cond = c * jax.nn.sigmoid(c)
    o_ref[0] = jnp.dot(cond, w_ref[0], preferred_element_type=F32, precision=HIGHEST) + b_ref[0]


def _ada_call(c_pad, ada_w, ada_b):
    depth, d, n = ada_w.shape
    tn = n // 6
    return pl.pallas_call(
        _ada_kernel,
        grid=(depth, n // tn),
        in_specs=[
            pl.BlockSpec((SUBLANES, d), lambda l, j: (0, 0)),
            pl.BlockSpec((1, d, tn), lambda l, j: (l, 0, j)),
            pl.BlockSpec((1, 1, tn), lambda l, j: (l, 0, j)),
        ],
        out_specs=pl.BlockSpec((1, SUBLANES, tn), lambda l, j: (l, 0, j)),
        out_shape=jax.ShapeDtypeStruct((depth, SUBLANES, n), F32),
        compiler_params=_cparams("arbitrary", "arbitrary"),
        name="ada",
    )(c_pad, ada_w, ada_b.reshape(depth, 1, n))


def _rope_lanes(z, cos, sin_signed, first_half):
    partner = jnp.where(first_half, pltpu.roll(z, LANES - 32, 1), pltpu.roll(z, 32, 1))
    return z * cos + partner * sin_signed


def _anti_identity(n):
    r = lax.broadcasted_iota(jnp.int32, (n, n), 0)
    c = lax.broadcasted_iota(jnp.int32, (n, n), 1)
    return (r + c == n - 1).astype(BF16)


def _pre_kernel(x_ref, sc_ref, sh_ref, w_ref, ca_ref, sa_ref, cr_ref, sr_ref,
                qa_ref, ka_ref, va_ref, ub_ref, ubr_ref, qr_ref, kr_ref, vr_ref, gr_ref):
    tm = x_ref.shape[1]
    h = (_ln(x_ref[0]) * (1.0 + sc_ref[0]) + sh_ref[0]).astype(BF16)
    lane = lax.broadcasted_iota(jnp.int32, (tm, LANES), 1)
    first_half = (lane % 64) < 32

    def proj(off, width):
        return jnp.dot(h, w_ref[:, off:off + width], preferred_element_type=F32)

    def rope_store(z, cos, sin, scale, out_ref):
        for j in range(z.shape[1] // LANES):
            zj = _rope_lanes(z[:, j * LANES:(j + 1) * LANES], cos, sin, first_half)
            out_ref[0, :, j * LANES:(j + 1) * LANES] = (zj * scale).astype(out_ref.dtype)

    ca, sa, cr, sr = ca_ref[0], sa_ref[0], cr_ref[0], sr_ref[0]
    rope_store(proj(OFF_QA, DIFF_QK), ca, sa, DIFF_HEAD_DIM ** -0.5 * math.log2(math.e), qa_ref)
    rope_store(proj(OFF_KA, DIFF_QK), ca, sa, 1.0, ka_ref)
    va_ref[0] = proj(OFF_VA, DIFF_WIDTH).astype(va_ref.dtype)
    u = proj(OFF_UB, S5_WIDTH)
    ub_ref[0] = u
    ubr_ref[0] = jnp.dot(_anti_identity(tm), u.astype(BF16), preferred_element_type=F32).astype(BF16)
    rope_store(proj(OFF_QR, RET_WIDTH), cr, sr, 1.0, qr_ref)
    rope_store(proj(OFF_KR, RET_WIDTH), cr, sr, RET_HEAD_DIM ** -0.5, kr_ref)
    vr_ref[0] = proj(OFF_VR, RET_WIDTH).astype(vr_ref.dtype)
    gr_ref[0] = proj(OFF_GR, RET_WIDTH).astype(gr_ref.dtype)


def _pre_call(x, sc, sh, w_in_bf, cos_a, sin_a, cos_r, sin_r, tm):
    b, l, d = x.shape
    row = lambda bi, i: (bi, i, 0)
    vec = lambda bi, i: (bi, 0, 0)
    nl = l // tm
    mirrored = lambda bi, i: (bi, nl - 1 - i, 0)
    widths = (DIFF_QK, DIFF_QK, DIFF_WIDTH, S5_WIDTH, S5_WIDTH, RET_WIDTH, RET_WIDTH, RET_WIDTH, RET_WIDTH)
    dtypes = (BF16, BF16, BF16, F32, BF16, BF16, BF16, BF16, BF16)
    maps = (row, row, row, row, mirrored, row, row, row, row)
    return pl.pallas_call(
        _pre_kernel,
        grid=(b, l // tm),
        in_specs=[
            pl.BlockSpec((1, tm, d), row),
            pl.BlockSpec((1, 1, d), vec),
            pl.BlockSpec((1, 1, d), vec),
            pl.BlockSpec((d, IN_WIDTH), lambda bi, i: (0, 0)),
            pl.BlockSpec((1, tm, LANES), row),
            pl.BlockSpec((1, tm, LANES), row),
            pl.BlockSpec((1, tm, LANES), row),
            pl.BlockSpec((1, tm, LANES), row),
        ],
        out_specs=[pl.BlockSpec((1, tm, w), m) for w, m in zip(widths, maps)],
        out_shape=[jax.ShapeDtypeStruct((b, l, w), dt) for w, dt in zip(widths, dtypes)],
        compiler_params=_cparams("arbitrary", "arbitrary"),
        name="pre",
    )(x, sc, sh, w_in_bf, cos_a, sin_a, cos_r, sin_r)


ONES_ROWS = 16


def _attn_kernel(qt_ref, k_ref, vt_ref, g_ref, lq1_ref, lk1_ref, lq2_ref, lk2_ref, o_ref, s_a, s_b, *, lam_init):
    tq = qt_ref.shape[2]
    nk, _, tk = vt_ref.shape[1:]
    qt = qt_ref[0]
    row = lax.broadcasted_iota(jnp.int32, qt.shape, 0)
    zero = jnp.zeros_like(qt)
    qq = jnp.concatenate([jnp.where(row < DIFF_HEAD_DIM, qt, zero), jnp.where(row >= DIFF_HEAD_DIM, qt, zero)],
                         axis=1)
    ones = jnp.ones((ONES_ROWS, tk), BF16)

    def scores(j, s_ref):
        start = pl.multiple_of(j * tk, tk)
        s = jnp.dot(k_ref[0, pl.ds(start, tk), :], qq, preferred_element_type=F32)
        s_ref[...] = s
        return jnp.max(s, axis=0, keepdims=True)

    def update(j, s_ref, mc, m, acc):
        m_new = jnp.maximum(m, mc)
        p = jnp.exp2(s_ref[...] - m_new).astype(BF16)
        vt = jnp.concatenate([vt_ref[0, j], ones], axis=0)
        return m_new, jnp.exp2(m - m_new) * acc + jnp.dot(vt, p, preferred_element_type=F32)

    def body(jj, carry):
        mc_a, m, acc = carry
        j = 2 * jj
        mc_b = scores(j + 1, s_b)
        m, acc = update(j, s_a, mc_a, m, acc)
        mc_a = scores(jnp.minimum(j + 2, nk - 1), s_a)
        m, acc = update(j + 1, s_b, mc_b, m, acc)
        return mc_a, m, acc

    m0 = jnp.full((1, 2 * tq), NEG_BIG, F32)
    a0 = jnp.zeros((DIFF_V_DIM + ONES_ROWS, 2 * tq), F32)
    _, _, acc = lax.fori_loop(0, nk // 2, body, (scores(0, s_a), m0, a0))
    lam = (jnp.exp(jnp.sum(lq1_ref[...] * lk1_ref[...], axis=1, keepdims=True))
           - jnp.exp(jnp.sum(lq2_ref[...] * lk2_ref[...], axis=1, keepdims=True)) + lam_init)
    on = acc[:DIFF_V_DIM] / acc[DIFF_V_DIM:DIFF_V_DIM + 1]
    ot = on[:, :tq] - lam * on[:, tq:]
    ot = ot * lax.rsqrt(jnp.mean(ot * ot, axis=0, keepdims=True) + LN_EPS)
    o_ref[0] = (ot.T * g_ref[...] * (1.0 - lam_init)).astype(o_ref.dtype)


def _attn_call(qa, ka, va, subln_g, lq1, lk1, lq2, lk2, lam_init, tq, tk):
    b, l, w = qa.shape
    nk = l // tk
    qt = qa.transpose(0, 2, 1)
    vt = va.reshape(b, nk, tk, w).transpose(0, 1, 3, 2)
    cmap = lambda bi, h, i: (0, 0)
    small = pl.BlockSpec((1, DIFF_HEAD_DIM), cmap)
    return pl.pallas_call(
        functools.partial(_attn_kernel, lam_init=lam_init),
        grid=(b, DIFF_HEADS, l // tq),
        in_specs=[
            pl.BlockSpec((1, LANES, tq), lambda bi, h, i: (bi, h, i)),
            pl.BlockSpec((1, l, LANES), lambda bi, h, i: (bi, 0, h)),
            pl.BlockSpec((1, nk, LANES, tk), lambda bi, h, i: (bi, 0, h, 0)),
            pl.BlockSpec((1, DIFF_V_DIM), cmap),
            small, small, small, small,
        ],
        out_specs=pl.BlockSpec((1, tq, LANES), lambda bi, h, i: (bi, i, h)),
        out_shape=jax.ShapeDtypeStruct((b, l, DIFF_WIDTH), BF16),
        scratch_shapes=[pltpu.VMEM((tk, 2 * tq), F32), pltpu.VMEM((tk, 2 * tq), F32)],
        compiler_params=_cparams("arbitrary", "arbitrary", "arbitrary"),
        name="attn",
    )(qt, ka, vt, subln_g, lq1, lk1, lq2, lk2)


def _s5_kernel(u_ref, bf_ref, bb_ref, c_ref, a_ref, y_ref, xs_ref, st_ref, *, tc, rb):
    n = S5_NS
    rows = tc * SUBLANES

    @pl.when(pl.program_id(0) == 0)
    def _():
        st_ref[...] = jnp.zeros_like(st_ref)

    def in_proj(i, c):
        r0 = pl.multiple_of(i * rb, rb)
        u = u_ref[pl.ds(r0, rb), :]
        fwd = (lax.broadcasted_iota(jnp.int32, u.shape, 0) % SUBLANES) < SUBLANES // 2
        zero = jnp.zeros_like(u)
        xs_ref[pl.ds(r0, rb), :] = (
            jnp.dot(jnp.where(fwd, u, zero), bf_ref[...], preferred_element_type=F32)
            + jnp.dot(jnp.where(fwd, zero, u), bb_ref[...], preferred_element_type=F32))
        return c

    lax.fori_loop(0, rows // rb, in_proj, 0)

    a_re = a_ref[:, :n]
    a_im = a_ref[:, n:]

    def step(t, carry):
        h_re, h_im = carry
        r0 = pl.multiple_of(t * SUBLANES, SUBLANES)
        x_re = xs_ref[pl.ds(r0, SUBLANES), :n]
        x_im = xs_ref[pl.ds(r0, SUBLANES), n:]
        n_re = a_re * h_re - a_im * h_im + x_re
        n_im = a_re * h_im + a_im * h_re + x_im
        xs_ref[pl.ds(r0, SUBLANES), :n] = n_re
        xs_ref[pl.ds(r0, SUBLANES), n:] = n_im
        return n_re, n_im

    h_re, h_im = lax.fori_loop(0, tc, step, (st_ref[:, :n], st_ref[:, n:]), unroll=4)
    st_ref[:, :n] = h_re
    st_ref[:, n:] = h_im

    def out_proj(i, c):
        r0 = pl.multiple_of(i * rb, rb)
        hs = xs_ref[pl.ds(r0, rb), :].astype(BF16)
        yy = jnp.dot(hs, c_ref[...], preferred_element_type=F32)
        fwd = (lax.broadcasted_iota(jnp.int32, (rb, S5_WIDTH), 0) % SUBLANES) < SUBLANES // 2
        y_ref[pl.ds(r0, rb), :] = jnp.where(fwd, yy[:, :S5_WIDTH], yy[:, S5_WIDTH:])
        return c

    lax.fori_loop(0, rows // rb, out_proj, 0)


def _s5_call(u_int, bmat_f, bmat_b, cmat, a_rows, tc):
    rows_total = u_int.shape[0]
    rows = tc * SUBLANES
    rb = min(rows, 256)
    full = lambda i: (0, 0)
    return pl.pallas_call(
        functools.partial(_s5_kernel, tc=tc, rb=rb),
        grid=(rows_total // rows,),
        in_specs=[
            pl.BlockSpec((rows, S5_WIDTH), lambda i: (i, 0)),
            pl.BlockSpec(bmat_f.shape, full),
            pl.BlockSpec(bmat_b.shape, full),
            pl.BlockSpec(cmat.shape, full),
            pl.BlockSpec(a_rows.shape, full),
        ],
        out_specs=pl.BlockSpec((rows, S5_WIDTH), lambda i: (i, 0)),
        out_shape=jax.ShapeDtypeStruct((rows_total, S5_WIDTH), F32),
        scratch_shapes=[pltpu.VMEM((rows, 2 * S5_NS), F32), pltpu.VMEM((SUBLANES, 2 * S5_NS), F32)],
        compiler_params=_cparams("arbitrary"),
        name="s5",
    )(u_int, bmat_f, bmat_b, cmat, a_rows)


def _s5_params(lam_re, lam_im, log_step, b_re, b_im, c_re, c_im, batch):
    lr = jnp.minimum(lam_re.astype(F32), -1e-4)
    li = lam_im.astype(F32)
    dt = jnp.exp(log_step.astype(F32))[..., None]
    mag = jnp.exp(lr * dt)
    ab_re = mag * jnp.cos(li * dt)
    ab_im = mag * jnp.sin(li * dt)
    den = lr * lr + li * li
    f_re = ((ab_re - 1.0) * lr + ab_im * li) / den
    f_im = (ab_im * lr - (ab_re - 1.0) * li) / den
    bre = b_re.astype(F32)
    bim = b_im.astype(F32)
    bb_re = f_re[..., None] * bre - f_im[..., None] * bim
    bb_im = f_re[..., None] * bim + f_im[..., None] * bre
    eye = jnp.eye(S5_GROUPS, dtype=F32)

    def in_mat(m):
        return jnp.einsum('gpn,gh->gnhp', m, eye).reshape(S5_WIDTH, S5_NS)

    def out_mat(m):
        return jnp.einsum('gnp,gh->gphn', m, eye).reshape(S5_NS, S5_WIDTH)

    bmats = [jnp.concatenate([in_mat(bb_re[d]), in_mat(bb_im[d])], axis=1).astype(BF16) for d in range(2)]
    cmat = jnp.concatenate(
        [jnp.concatenate([out_mat(c_re[d].astype(F32)), -out_mat(c_im[d].astype(F32))], axis=0) for d in range(2)],
        axis=1).astype(BF16)
    a_dir = jnp.concatenate([ab_re.reshape(2, S5_NS), ab_im.reshape(2, S5_NS)], axis=1)
    a_rows = jnp.repeat(a_dir, batch, axis=0)
    return bmats[0], bmats[1], cmat, a_rows


def _ret_kernel(q_ref, k_ref, v_ref, g_ref, dm_ref, wf_ref, wb_ref, gf_ref, gb_ref, dec_ref,
                o_ref, y_ref, *, chunk):
    seq = q_ref.shape[1]
    n = seq // chunk
    lane = lax.broadcasted_iota(jnp.int32, (chunk, LANES), 1)
    head_lanes = [lane < RET_HEAD_DIM, lane >= RET_HEAD_DIM]
    nt = (((1,), (1,)), ((), ()))
    tn = (((0,), (0,)), ((), ()))

    def load(ref, c0):
        return ref[0, pl.ds(c0, chunk), :]

    def fwd(i, states):
        c0 = pl.multiple_of(i * chunk, chunk)
        qc, kc, vc = load(q_ref, c0), load(k_ref, c0), load(v_ref, c0)
        y = jnp.zeros((chunk, LANES), F32)
        new_states = []
        for h in range(2):
            qm = jnp.where(head_lanes[h], qc, jnp.zeros_like(qc))
            s = lax.dot_general(qm, kc, nt, preferred_element_type=F32) * dm_ref[h]
            yh = jnp.dot(s.astype(BF16), vc, preferred_element_type=F32)
            yh = yh + jnp.dot(qm, states[h].astype(BF16), preferred_element_type=F32) * gf_ref[h]
            kw = (kc.astype(F32) * wf_ref[h]).astype(BF16)
            new_states.append(dec_ref[h, 0:1, :] * states[h]
                              + lax.dot_general(kw, vc, tn, preferred_element_type=F32))
            y = jnp.where(head_lanes[h], yh, y)
        y_ref[pl.ds(c0, chunk), :] = y
        return tuple(new_states)

    z = jnp.zeros((LANES, LANES), F32)
    lax.fori_loop(0, n, fwd, (z, z))

    def bwd(ii, states):
        i = n - 1 - ii
        c0 = pl.multiple_of(i * chunk, chunk)
        qc, kc, vc = load(q_ref, c0), load(k_ref, c0), load(v_ref, c0)
        y = y_ref[pl.ds(c0, chunk), :]
        new_states = []
        for h in range(2):
            qm = jnp.where(head_lanes[h], qc, jnp.zeros_like(qc))
            yh = jnp.dot(qm, states[h].astype(BF16), preferred_element_type=F32) * gb_ref[h]
            kw = (kc.astype(F32) * wb_ref[h]).astype(BF16)
            new_states.append(dec_ref[h, 1:2, :] * states[h]
                              + lax.dot_general(kw, vc, tn, preferred_element_type=F32))
            y = y + jnp.where(head_lanes[h], yh, 0.0)
        inv = 1.0 / RET_HEAD_DIM
        mu = [jnp.sum(jnp.where(m, y, 0.0), axis=1, keepdims=True) * inv for m in head_lanes]
        yc = y - jnp.where(head_lanes[0], mu[0], mu[1])
        var = [jnp.sum(jnp.where(m, yc * yc, 0.0), axis=1, keepdims=True) * inv for m in head_lanes]
        yn = yc * lax.rsqrt(jnp.where(head_lanes[0], var[0], var[1]) + LN_EPS)
        g = load(g_ref, c0).astype(F32)
        o_ref[0, pl.ds(c0, chunk), :] = (g * jax.nn.sigmoid(g) * yn).astype(o_ref.dtype)
        return tuple(new_states)

    lax.fori_loop(0, n, bwd, (z, z))


def _ret_tables(chunk):
    hidx = jnp.arange(RET_HEADS, dtype=F32)
    log_gf = jnp.log1p(-jnp.exp2(-5.0 - hidx))
    log_gb = jnp.log1p(-jnp.exp2(-5.5 - hidx))
    j = jnp.arange(chunk, dtype=F32)
    rel = j[:, None] - j[None, :]
    dmask = jnp.exp(jnp.where(rel >= 0, log_gf[:, None, None], log_gb[:, None, None]) * jnp.abs(rel))
    rows = lambda t: jnp.broadcast_to(t[:, :, None], (RET_HEADS, chunk, LANES))
    wf = rows(jnp.exp(log_gf[:, None] * (chunk - 1.0 - j)))
    wb = rows(jnp.exp(log_gb[:, None] * j))
    gf = rows(jnp.exp(log_gf[:, None] * (j + 1.0)))
    gb = rows(jnp.exp(log_gb[:, None] * (chunk - j)))
    dec = jnp.stack([jnp.exp(log_gf * chunk), jnp.exp(log_gb * chunk)], axis=1)
    dec = jnp.broadcast_to(jnp.pad(dec, ((0, 0), (0, SUBLANES - 2)))[:, :, None], (RET_HEADS, SUBLANES, LANES))
    return dmask, wf, wb, gf, gb, dec


def _ret_call(qr, kr, vr, gr, tables, chunk):
    b, l, _ = qr.shape
    dmask, wf, wb, gf, gb, dec = tables
    seq = lambda bi, p: (bi, 0, p)
    pair = lambda bi, p: (p, 0, 0)
    seq_spec = pl.BlockSpec((1, l, LANES), seq)
    row_spec = pl.BlockSpec((2, chunk, LANES), pair)
    return pl.pallas_call(
        functools.partial(_ret_kernel, chunk=chunk),
        grid=(b, RET_HEADS // 2),
        in_specs=[seq_spec, seq_spec, seq_spec, seq_spec,
                  pl.BlockSpec((2, chunk, chunk), pair),
                  row_spec, row_spec, row_spec, row_spec,
                  pl.BlockSpec((2, SUBLANES, LANES), pair)],
        out_specs=seq_spec,
        out_shape=jax.ShapeDtypeStruct((b, l, RET_WIDTH), BF16),
        scratch_shapes=[pltpu.VMEM((l, LANES), F32)],
        compiler_params=_cparams("arbitrary", "arbitrary"),
        name="ret",
    )(qr, kr, vr, gr, dmask, wf, wb, gf, gb, dec)


def _post_kernel(ya_ref, yf_ref, yb_ref, ub_ref, yc_ref, x_ref, dsk_ref, gw_ref, gb_ref,
                 woa_ref, wob_ref, woc_ref, g1_ref, lng_ref, lnb_ref, sc2_ref, sh2_ref, rwt_ref, rb_ref,
                 x1_ref, h2_ref, idx_ref, gate_ref, *, alpha):
    ybw = yb_ref[0]
    hi = ybw.astype(BF16)
    lo = (ybw - hi.astype(F32)).astype(BF16)
    rev = _anti_identity(ybw.shape[0])
    ybw = jnp.dot(rev, hi, preferred_element_type=F32) + jnp.dot(rev, lo, preferred_element_type=F32)
    y = yf_ref[0] + ybw + dsk_ref[...] * ub_ref[0]
    y = jax.nn.gelu(y).astype(BF16)
    z = jnp.dot(y, gw_ref[...], preferred_element_type=F32) + gb_ref[...]
    yb = (z[:, :S5_WIDTH] * jax.nn.sigmoid(z[:, S5_WIDTH:])).astype(BF16)
    mix = (jnp.dot(ya_ref[0], woa_ref[...], preferred_element_type=F32)
           + jnp.dot(yb, wob_ref[...], preferred_element_type=F32)
           + jnp.dot(yc_ref[0], woc_ref[...], preferred_element_type=F32))
    x1 = _ln(alpha * x_ref[0] + (1.0 + g1_ref[0]) * mix) * lng_ref[...] + lnb_ref[...]
    x1_ref[0] = x1
    h2 = _ln(x1) * (1.0 + sc2_ref[0]) + sh2_ref[0]
    _store_rows_as_tiles(h2_ref, h2)
    logits = lax.dot_general(rwt_ref[...], h2, (((1,), (1,)), ((), ())),
                             preferred_element_type=F32, precision=HIGHEST) + rb_ref[...]
    eidx = lax.broadcasted_iota(jnp.int32, logits.shape, 0)
    vals, idxs = [], []
    cur = logits
    for _ in range(TOP_K):
        m = jnp.max(cur, axis=0, keepdims=True)
        i = jnp.min(jnp.where(cur == m, eidx, N_EXPERTS), axis=0, keepdims=True)
        vals.append(m)
        idxs.append(i)
        cur = jnp.where(eidx == i, -jnp.inf, cur)
    tv = jnp.concatenate(vals, axis=0)
    e = jnp.exp(tv - vals[0])
    gate_ref[...] = e / jnp.sum(e, axis=0, keepdims=True)
    idx_ref[...] = jnp.concatenate(idxs, axis=0)


def _post_call(ya, yf, yb, ub, yc, x, dsk, gw, gb, woa, wob, woc, g1, lng, lnb, sc2, sh2, rwt, rb, alpha, tm):
    b, l, d = x.shape
    row = lambda bi, i: (bi, i, 0)
    vec = lambda bi, i: (bi, 0, 0)
    full2 = lambda bi, i: (0, 0)
    nl = l // tm
    tok = lambda bi, i: (0, bi * nl + i)

    def fs(a):
        return pl.BlockSpec(a.shape, full2)

    return pl.pallas_call(
        functools.partial(_post_kernel, alpha=alpha),
        grid=(b, nl),
        in_specs=[
            pl.BlockSpec((1, tm, DIFF_WIDTH), row),
            pl.BlockSpec((1, tm, S5_WIDTH), row),
            pl.BlockSpec((1, tm, S5_WIDTH), lambda bi, i: (bi, nl - 1 - i, 0)),
            pl.BlockSpec((1, tm, S5_WIDTH), row),
            pl.BlockSpec((1, tm, RET_WIDTH), row),
            pl.BlockSpec((1, tm, d), row),
            fs(dsk), fs(gw), fs(gb), fs(woa), fs(wob), fs(woc),
            pl.BlockSpec((1, 1, d), vec),
            fs(lng), fs(lnb),
            pl.BlockSpec((1, 1, d), vec),
            pl.BlockSpec((1, 1, d), vec),
            fs(rwt), fs(rb),
        ],
        out_specs=[
            pl.BlockSpec((1, tm, d), row),
            pl.BlockSpec((tm * TOK_TILE, LANES), lambda bi, i: (bi * nl + i, 0)),
            pl.BlockSpec((TOP_K, tm), tok),
            pl.BlockSpec((TOP_K, tm), tok),
        ],
        out_shape=[
            jax.ShapeDtypeStruct((b, l, d), F32),
            jax.ShapeDtypeStruct((b * l * TOK_TILE, LANES), F32),
            jax.ShapeDtypeStruct((TOP_K, b * l), jnp.int32),
            jax.ShapeDtypeStruct((TOP_K, b * l), F32),
        ],
        compiler_params=_cparams("arbitrary", "arbitrary"),
        name="post",
    )(ya, yf, yb, ub, yc, x, dsk, gw, gb, woa, wob, woc, g1, lng, lnb, sc2, sh2, rwt, rb)


TOK_TILE = D_MODEL // LANES


def _tiles_to_rows(ref, lead, first, n):
    return jnp.concatenate(
        [ref[lead, pl.ds(first * TOK_TILE + s, n, stride=TOK_TILE), :] for s in range(TOK_TILE)], axis=1)


def _store_rows_as_tiles(ref, x):
    n = x.shape[0]
    for s in range(TOK_TILE):
        ref[pl.ds(s, n, stride=TOK_TILE), :] = x[:, s * LANES:(s + 1) * LANES]


def _gather_start(src_hbm, row_ref, n, dst, slot, sem, dst_token):
    for r in range(n):
        src = pl.multiple_of(row_ref[0, 0, r], TOK_TILE)
        pltpu.make_async_copy(src_hbm.at[pl.ds(src, TOK_TILE), :],
                              dst.at[slot, pl.ds(dst_token(r) * TOK_TILE, TOK_TILE), :], sem.at[slot]).start()


def _gather_wait(dst, slot, sem):
    pltpu.make_async_copy(dst.at[1 - slot], dst.at[slot], sem.at[slot]).wait()


def _expert_kernel(be_ref, nu_ref, row_ref, rown_ref, h_hbm, w1_ref, b1_ref, w2_ref, b2_ref, o_ref, buf, sem):
    eb = row_ref.shape[2]
    ff = w2_ref.shape[1]
    i = pl.program_id(0)
    last = pl.num_programs(0) - 1
    slot = i % 2

    @pl.when(i == 0)
    def _():
        _gather_start(h_hbm, row_ref, eb, buf, 0, sem, lambda r: r)

    _gather_wait(buf, slot, sem)

    def prefetch():
        _gather_start(h_hbm, rown_ref, eb, buf, 1 - slot, sem, lambda r: r)

    @pl.when(i < nu_ref[0])
    def _():
        hb = _tiles_to_rows(buf, slot, 0, eb).astype(BF16)
        prefetch()
        z = jnp.dot(hb, w1_ref[0], preferred_element_type=F32) + b1_ref[0]
        acts = []
        for j in range(ff // LANES):
            z_glu = jnp.minimum(z[:, 2 * j * LANES:(2 * j + 1) * LANES], SWIGLU_LIMIT)
            z_lin = jnp.clip(z[:, (2 * j + 1) * LANES:(2 * j + 2) * LANES], -SWIGLU_LIMIT, SWIGLU_LIMIT)
            acts.append(z_glu * jax.nn.sigmoid(SWIGLU_ALPHA * z_glu) * (z_lin + 1.0))
        act = jnp.concatenate(acts, axis=1).astype(BF16)
        _store_rows_as_tiles(o_ref, jnp.dot(act, w2_ref[0], preferred_element_type=F32) + b2_ref[0])

    @pl.when(i >= nu_ref[0])
    def _():
        prefetch()
        o_ref[...] = jnp.zeros_like(o_ref)

    @pl.when(i == last)
    def _():
        _gather_wait(buf, 1 - slot, sem)


def _expert_call(block_e, n_used, slot_row, h2_tiles, w1, b1, w2, b2):
    n_blocks = block_e.shape[0]
    d = w1.shape[1]
    ff = w2.shape[1]
    eb = EXPERT_BLOCK
    rows3 = slot_row.reshape(n_blocks, 1, eb)
    grid_spec = pltpu.PrefetchScalarGridSpec(
        num_scalar_prefetch=2,
        grid=(n_blocks,),
        in_specs=[
            pl.BlockSpec((1, 1, eb), lambda i, be, nu: (i, 0, 0), memory_space=pltpu.SMEM),
            pl.BlockSpec((1, 1, eb), lambda i, be, nu: (jnp.minimum(i + 1, n_blocks - 1), 0, 0),
                         memory_space=pltpu.SMEM),
            pl.BlockSpec(memory_space=pl.ANY),
            pl.BlockSpec((1, d, 2 * ff), lambda i, be, nu: (be[i], 0, 0)),
            pl.BlockSpec((1, 1, 2 * ff), lambda i, be, nu: (be[i], 0, 0)),
            pl.BlockSpec((1, ff, d), lambda i, be, nu: (be[i], 0, 0)),
            pl.BlockSpec((1, 1, d), lambda i, be, nu: (be[i], 0, 0)),
        ],
        out_specs=pl.BlockSpec((eb * TOK_TILE, LANES), lambda i, be, nu: (i, 0)),
        scratch_shapes=[pltpu.VMEM((2, eb * TOK_TILE, LANES), F32), pltpu.SemaphoreType.DMA((2,))],
    )
    return pl.pallas_call(
        _expert_kernel,
        grid_spec=grid_spec,
        out_shape=jax.ShapeDtypeStruct((n_blocks * eb * TOK_TILE, LANES), F32),
        compiler_params=_cparams("arbitrary"),
        name="expert",
    )(block_e, n_used, rows3, rows3, h2_tiles, w1, b1, w2, b2)


def _w1prep_kernel(w_ref, p_ref, o_ref):
    for j in range(w_ref.shape[1] // (2 * LANES)):
        cols = slice(2 * j * LANES, 2 * (j + 1) * LANES)
        o_ref[:, cols] = jnp.dot(w_ref[:, cols].astype(BF16), p_ref[...], preferred_element_type=F32).astype(BF16)


def _w1prep_call(w1_rows, tr):
    n, c = w1_rows.shape
    src = jnp.arange(2 * LANES, dtype=jnp.int32)
    src = jnp.where(src < LANES, 2 * src, 2 * (src - LANES) + 1)
    perm = (jnp.arange(2 * LANES, dtype=jnp.int32)[:, None] == src[None, :]).astype(BF16)
    return pl.pallas_call(
        _w1prep_kernel,
        grid=(n // tr,),
        in_specs=[pl.BlockSpec((tr, c), lambda i: (i, 0)), pl.BlockSpec(perm.shape, lambda i: (0, 0))],
        out_specs=pl.BlockSpec((tr, c), lambda i: (i, 0)),
        out_shape=jax.ShapeDtypeStruct((n, c), BF16),
        compiler_params=_cparams("arbitrary"),
        name="w1prep",
    )(w1_rows, perm)


def _combine_kernel(row_ref, rown_ref, ys_hbm, gate_ref, x1_ref, g2_ref, lng_ref, lnb_ref, o_ref, buf, sem, *, alpha):
    tm = x1_ref.shape[0]
    i = pl.program_id(0)
    slot = i % 2
    place = lambda r: (r % TOP_K) * tm + r // TOP_K

    @pl.when(i == 0)
    def _():
        _gather_start(ys_hbm, row_ref, tm * TOP_K, buf, 0, sem, place)

    _gather_wait(buf, slot, sem)
    gates = gate_ref[...]
    ffo = _tiles_to_rows(buf, slot, 0, tm) * gates[:, 0:1]
    for k in range(1, TOP_K):
        ffo = ffo + _tiles_to_rows(buf, slot, k * tm, tm) * gates[:, k:k + 1]
    _gather_start(ys_hbm, rown_ref, tm * TOP_K, buf, 1 - slot, sem, place)
    o_ref[...] = _ln(alpha * x1_ref[...] + (1.0 + g2_ref[0]) * ffo) * lng_ref[...] + lnb_ref[...]

    @pl.when(i == pl.num_programs(0) - 1)
    def _():
        _gather_wait(buf, 1 - slot, sem)


def _combine_call(dest_row, ys_tiles, gates_t, x1, g2, lng, lnb, alpha, tm):
    b, l, d = x1.shape
    nl = l // tm
    n_tiles = b * nl
    rows3 = dest_row.reshape(n_tiles, 1, tm * TOP_K)
    full2 = lambda i: (0, 0)
    return pl.pallas_call(
        functools.partial(_combine_kernel, alpha=alpha),
        grid=(n_tiles,),
        in_specs=[
            pl.BlockSpec((1, 1, tm * TOP_K), lambda i: (i, 0, 0), memory_space=pltpu.SMEM),
            pl.BlockSpec((1, 1, tm * TOP_K), lambda i: (jnp.minimum(i + 1, n_tiles - 1), 0, 0),
                         memory_space=pltpu.SMEM),
            pl.BlockSpec(memory_space=pl.ANY),
            pl.BlockSpec((tm, TOP_K), lambda i: (i, 0)),
            pl.BlockSpec((tm, d), lambda i: (i, 0)),
            pl.BlockSpec((1, 1, d), lambda i: (i // nl, 0, 0)),
            pl.BlockSpec(lng.shape, full2),
            pl.BlockSpec(lnb.shape, full2),
        ],
        out_specs=pl.BlockSpec((tm, d), lambda i: (i, 0)),
        out_shape=jax.ShapeDtypeStruct((b * l, d), F32),
        scratch_shapes=[pltpu.VMEM((2, TOP_K * tm * TOK_TILE, LANES), F32), pltpu.SemaphoreType.DMA((2,))],
        compiler_params=_cparams("arbitrary"),
        name="combine",
    )(rows3, rows3, ys_tiles, gates_t, x1.reshape(b * l, d), g2, lng, lnb).reshape(b, l, d)


def _route(top_idx, n_tok):
    n_assign = n_tok * TOP_K
    n_blocks = n_assign // EXPERT_BLOCK + N_EXPERTS
    e_flat = top_idx.T.reshape(-1)
    order = jnp.argsort(e_flat)
    e_sorted = e_flat[order]
    counts = jnp.sum((e_flat[:, None] == jnp.arange(N_EXPERTS, dtype=jnp.int32)[None, :]).astype(jnp.int32), axis=0)
    start = jnp.cumsum(counts) - counts
    padded = (counts + EXPERT_BLOCK - 1) // EXPERT_BLOCK * EXPERT_BLOCK
    pend = jnp.cumsum(padded)
    pstart = pend - padded
    dest_sorted = pstart[e_sorted] + jnp.arange(n_assign, dtype=jnp.int32) - start[e_sorted]
    _, dest = lax.sort((order.astype(jnp.int32), dest_sorted.astype(jnp.int32)), num_keys=1)
    block_start = jnp.arange(n_blocks, dtype=jnp.int32) * EXPERT_BLOCK
    block_e = jnp.minimum(jnp.searchsorted(pend, block_start, side='right'), N_EXPERTS - 1).astype(jnp.int32)
    slot = jnp.arange(n_blocks * EXPERT_BLOCK, dtype=jnp.int32)
    slot_e = jnp.repeat(block_e, EXPERT_BLOCK)
    within = slot - pstart[slot_e]
    valid = (within < counts[slot_e]) & (slot < pend[-1])
    src = jnp.clip(start[slot_e] + within, 0, n_assign - 1)
    slot_tok = jnp.where(valid, (order[src] // TOP_K).astype(jnp.int32), 0)
    n_used = (pend[-1] // EXPERT_BLOCK).astype(jnp.int32).reshape(1)
    return block_e, n_used, slot_tok, dest


def _pick(n, pref):
    return pref if n % pref == 0 else n


def kernel(x, c, positions, ada_w, ada_b, w_in, w_out, diff_lambda_q1, diff_lambda_k1, diff_lambda_q2,
           diff_lambda_k2, diff_subln_g, s5_lambda_re, s5_lambda_im, s5_log_step, s5_b_re, s5_b_im, s5_c_re,
           s5_c_im, s5_d, s5_glu_w, s5_glu_b, ln_g, ln_b, router_w, router_b, exp_w1, exp_b1, exp_w2, exp_b2):
    bsz, seq, d = x.shape
    depth = ada_w.shape[0]
    n_tok = bsz * seq
    alpha = (2 * depth) ** 0.25

    def tables(inv_freq):
        ang = positions.astype(F32)[..., None] * inv_freq
        cs, sn = jnp.cos(ang), jnp.sin(ang)
        return jnp.concatenate([cs, cs, cs, cs], axis=-1), jnp.concatenate([-sn, sn, -sn, sn], axis=-1)

    cos_a, sin_a = tables(1.0 / (ROPE_THETA ** (jnp.arange(0, DIFF_HEAD_DIM, 2, dtype=F32) / DIFF_HEAD_DIM)))
    cos_r, sin_r = tables(1.0 / (ROPE_THETA ** jnp.linspace(0.0, 1.0, RET_HEAD_DIM // 2, dtype=F32)))
    ret_tables = _ret_tables(RET_CHUNK)

    c_pad = jnp.pad(c.astype(F32), ((0, SUBLANES - bsz), (0, 0)))
    mod = _ada_call(c_pad, ada_w, ada_b)[:, :bsz]

    tm_pre = _pick(seq, 512)
    tq = _pick(seq, 256)
    tk = 512 if seq % 1024 == 0 else seq // 2
    tc = _pick(seq, 128)
    tm_post = _pick(seq, 256)
    tm_comb = _pick(seq, 128)

    for l in range(depth):
        sh1, sc1, g1, sh2, sc2, g2 = [mod[l, :, None, i * d:(i + 1) * d] for i in range(6)]
        qa, ka, va, ub, ubr, qr, kr, vr, gr = _pre_call(
            x, sc1, sh1, w_in[l].astype(BF16), cos_a, sin_a, cos_r, sin_r, tm_pre)

        lam_init = 0.8 - 0.6 * math.exp(-0.3 * l)
        ya = _attn_call(qa, ka, va, diff_subln_g[l][None, :].astype(F32),
                        diff_lambda_q1[l][None, :], diff_lambda_k1[l][None, :],
                        diff_lambda_q2[l][None, :], diff_lambda_k2[l][None, :], lam_init, tq, tk)

        u_seq = jnp.concatenate([ub.astype(BF16), ubr], axis=0)
        u_int = u_seq.transpose(1, 0, 2).reshape(seq * 2 * bsz, S5_WIDTH)
        bmat_f, bmat_b, cmat, a_rows = _s5_params(
            s5_lambda_re[l], s5_lambda_im[l], s5_log_step[l], s5_b_re[l], s5_b_im[l], s5_c_re[l], s5_c_im[l], bsz)
        y_int = _s5_call(u_int, bmat_f, bmat_b, cmat, a_rows, tc).reshape(seq, 2 * bsz, S5_WIDTH)
        y_fwd = y_int[:, :bsz].transpose(1, 0, 2)
        y_bwd = y_int[:, bsz:].transpose(1, 0, 2)

        yc = _ret_call(qr, kr, vr, gr, ret_tables, RET_CHUNK)

        wo = w_out[l].astype(BF16)
        x1, h2, top_idx, gates = _post_call(
            ya, y_fwd, y_bwd, ub, yc, x,
            s5_d[l][None, :].astype(F32), s5_glu_w[l].astype(BF16), s5_glu_b[l][None, :].astype(F32),
            wo[:DIFF_WIDTH], wo[DIFF_WIDTH:DIFF_WIDTH + S5_WIDTH], wo[DIFF_WIDTH + S5_WIDTH:],
            g1, ln_g[l, 0][None, :], ln_b[l, 0][None, :], sc2, sh2,
            router_w[l].T.astype(F32), router_b[l][:, None].astype(F32), alpha, tm_post)

        block_e, n_used, slot_tok, dest = _route(top_idx, n_tok)
        n_e, _, two_ff = exp_w1[l].shape
        w1 = _w1prep_call(exp_w1[l].reshape(n_e * d, two_ff), 512).reshape(n_e, d, two_ff)
        b1 = exp_b1[l].reshape(n_e, two_ff // (2 * LANES), LANES, 2).transpose(0, 1, 3, 2)
        b1 = b1.reshape(n_e, 1, two_ff).astype(F32)
        ys = _expert_call(block_e, n_used, slot_tok * TOK_TILE, h2, w1, b1,
                          exp_w2[l].astype(BF16), exp_b2[l][:, None, :].astype(F32))
        x = _combine_call(dest * TOK_TILE, ys, gates.T, x1, g2, ln_g[l, 1][None, :], ln_b[l, 1][None, :],
                          alpha, tm_comb)
    return x
```

```python
import functools
import math

import jax
import jax.numpy as jnp
from jax import lax
from jax.experimental import pallas as pl
from jax.experimental.pallas import tpu as pltpu

F32 = jnp.float32
BF16 = jnp.bfloat16
HIGHEST = lax.Precision.HIGHEST

LANES = 128
SUBLANES = 8
VMEM_LIMIT = 56 * 1024 * 1024

D_MODEL = 1024
DIFF_HEADS = 4
DIFF_WIDTH = D_MODEL // 2
DIFF_V_DIM = DIFF_WIDTH // DIFF_HEADS
DIFF_HEAD_DIM = DIFF_V_DIM // 2
DIFF_QK = DIFF_HEADS * 2 * DIFF_HEAD_DIM
ROPE_THETA = 10000.0
S5_WIDTH = D_MODEL // 4
S5_GROUP = 16
S5_GROUPS = S5_WIDTH // S5_GROUP
S5_STATE = 64
S5_NS = S5_GROUPS * S5_STATE
RET_HEADS = 4
RET_WIDTH = D_MODEL // 4
RET_HEAD_DIM = RET_WIDTH // RET_HEADS
RET_CHUNK = 128
IN_WIDTH = 2 * DIFF_QK + DIFF_WIDTH + S5_WIDTH + 4 * RET_WIDTH
N_EXPERTS = 32
TOP_K = 4
EXPERT_FF = D_MODEL
SWIGLU_ALPHA = 1.702
SWIGLU_LIMIT = 7.0
EXPERT_BLOCK = 256
LN_EPS = 1e-5
NEG_BIG = -1e30

OFF_QA = 0
OFF_KA = OFF_QA + DIFF_QK
OFF_VA = OFF_KA + DIFF_QK
OFF_UB = OFF_VA + DIFF_WIDTH
OFF_QR = OFF_UB + S5_WIDTH
OFF_KR = OFF_QR + RET_WIDTH
OFF_VR = OFF_KR + RET_WIDTH
OFF_GR = OFF_VR + RET_WIDTH


def _cparams(*sem):
    return pltpu.CompilerParams(dimension_semantics=sem, vmem_limit_bytes=VMEM_LIMIT)


def _ln(x):
    mu = jnp.mean(x, axis=-1, keepdims=True)
    xc = x - mu
    var = jnp.mean(xc * xc, axis=-1, keepdims=True)
    return xc * lax.rsqrt(var + LN_EPS)


def _ada_kernel(c_ref, w_ref, b_ref, o_ref):
    c = c_ref[...]
    cond = c * jax.nn.sigmoid(c)
    o_ref[0] = jnp.dot(cond, w_ref[0], preferred_element_type=F32, precision=HIGHEST) + b_ref[0]


def _ada_call(c_pad, ada_w, ada_b):
    depth, d, n = ada_w.shape
    tn = n // 6
    return pl.pallas_call(
        _ada_kernel,
        grid=(depth, n // tn),
        in_specs=[
            pl.BlockSpec((SUBLANES, d), lambda l, j: (0, 0)),
            pl.BlockSpec((1, d, tn), lambda l, j: (l, 0, j)),
            pl.BlockSpec((1, 1, tn), lambda l, j: (l, 0, j)),
        ],
        out_specs=pl.BlockSpec((1, SUBLANES, tn), lambda l, j: (l, 0, j)),
        out_shape=jax.ShapeDtypeStruct((depth, SUBLANES, n), F32),
        compiler_params=_cparams("arbitrary", "arbitrary"),
        name="ada",
    )(c_pad, ada_w, ada_b.reshape(depth, 1, n))


def _rope_lanes(z, cos, sin_signed, first_half):
    partner = jnp.where(first_half, pltpu.roll(z, LANES - 32, 1), pltpu.roll(z, 32, 1))
    return z * cos + partner * sin_signed


def _anti_identity(n):
    r = lax.broadcasted_iota(jnp.int32, (n, n), 0)
    c = lax.broadcasted_iota(jnp.int32, (n, n), 1)
    return (r + c == n - 1).astype(BF16)


def _pre_kernel(x_ref, sc_ref, sh_ref, w_ref, ca_ref, sa_ref, cr_ref, sr_ref,
                qa_ref, ka_ref, va_ref, ub_ref, ubr_ref, qr_ref, kr_ref, vr_ref, gr_ref):
    tm = x_ref.shape[1]
    h = (_ln(x_ref[0]) * (1.0 + sc_ref[0]) + sh_ref[0]).astype(BF16)
    lane = lax.broadcasted_iota(jnp.int32, (tm, LANES), 1)
    first_half = (lane % 64) < 32

    def proj(off, width):
        return jnp.dot(h, w_ref[:, off:off + width], preferred_element_type=F32)

    def rope_store(z, cos, sin, scale, out_ref):
        for j in range(z.shape[1] // LANES):
            zj = _rope_lanes(z[:, j * LANES:(j + 1) * LANES], cos, sin, first_half)
            out_ref[0, :, j * LANES:(j + 1) * LANES] = (zj * scale).astype(out_ref.dtype)

    ca, sa, cr, sr = ca_ref[0], sa_ref[0], cr_ref[0], sr_ref[0]
    rope_store(proj(OFF_QA, DIFF_QK), ca, sa, DIFF_HEAD_DIM ** -0.5 * math.log2(math.e), qa_ref)
    rope_store(proj(OFF_KA, DIFF_QK), ca, sa, 1.0, ka_ref)
    va_ref[0] = proj(OFF_VA, DIFF_WIDTH).astype(va_ref.dtype)
    u = proj(OFF_UB, S5_WIDTH)
    ub_ref[0] = u
    ubr_ref[0] = jnp.dot(_anti_identity(tm), u.astype(BF16), preferred_element_type=F32).astype(BF16)
    rope_store(proj(OFF_QR, RET_WIDTH), cr, sr, 1.0, qr_ref)
    rope_store(proj(OFF_KR, RET_WIDTH), cr, sr, RET_HEAD_DIM ** -0.5, kr_ref)
    vr_ref[0] = proj(OFF_VR, RET_WIDTH).astype(vr_ref.dtype)
    gr_ref[0] = proj(OFF_GR, RET_WIDTH).astype(gr_ref.dtype)


def _pre_call(x, sc, sh, w_in_bf, cos_a, sin_a, cos_r, sin_r, tm):
    b, l, d = x.shape
    row = lambda bi, i: (bi, i, 0)
    vec = lambda bi, i: (bi, 0, 0)
    nl = l // tm
    mirrored = lambda bi, i: (bi, nl - 1 - i, 0)
    widths = (DIFF_QK, DIFF_QK, DIFF_WIDTH, S5_WIDTH, S5_WIDTH, RET_WIDTH, RET_WIDTH, RET_WIDTH, RET_WIDTH)
    dtypes = (BF16, BF16, BF16, F32, BF16, BF16, BF16, BF16, BF16)
    maps = (row, row, row, row, mirrored, row, row, row, row)
    return pl.pallas_call(
        _pre_kernel,
        grid=(b, l // tm),
        in_specs=[
            pl.BlockSpec((1, tm, d), row),
            pl.BlockSpec((1, 1, d), vec),
            pl.BlockSpec((1, 1, d), vec),
            pl.BlockSpec((d, IN_WIDTH), lambda bi, i: (0, 0)),
            pl.BlockSpec((1, tm, LANES), row),
            pl.BlockSpec((1, tm, LANES), row),
            pl.BlockSpec((1, tm, LANES), row),
            pl.BlockSpec((1, tm, LANES), row),
        ],
        out_specs=[pl.BlockSpec((1, tm, w), m) for w, m in zip(widths, maps)],
        out_shape=[jax.ShapeDtypeStruct((b, l, w), dt) for w, dt in zip(widths, dtypes)],
        compiler_params=_cparams("arbitrary", "arbitrary"),
        name="pre",
    )(x, sc, sh, w_in_bf, cos_a, sin_a, cos_r, sin_r)


ATTN_CHUNKS_PER_TRIP = 8
ONES_ROWS = 16


def _attn_kernel(qt_ref, k_ref, vt_ref, g_ref, lq1_ref, lk1_ref, lq2_ref, lk2_ref, o_ref, s_a, s_b, *, lam_init):
    tq = qt_ref.shape[2]
    nk, _, tk = vt_ref.shape[1:]
    qt = qt_ref[0]
    row = lax.broadcasted_iota(jnp.int32, qt.shape, 0)
    zero = jnp.zeros_like(qt)
    qq = jnp.concatenate([jnp.where(row < DIFF_HEAD_DIM, qt, zero), jnp.where(row >= DIFF_HEAD_DIM, qt, zero)],
                         axis=1)
    ones = jnp.ones((ONES_ROWS, tk), BF16)

    def scores(j, s_ref):
        start = pl.multiple_of(j * tk, tk)
        s = jnp.dot(k_ref[0, pl.ds(start, tk), :], qq, preferred_element_type=F32)
        s_ref[...] = s
        return jnp.max(s, axis=0, keepdims=True)

    def update(j, s_ref, mc, m, acc):
        m_new = jnp.maximum(m, mc)
        p = jnp.exp2(s_ref[...] - m_new).astype(BF16)
        vt = jnp.concatenate([vt_ref[0, j], ones], axis=0)
        return m_new, jnp.exp2(m - m_new) * acc + jnp.dot(vt, p, preferred_element_type=F32)

    per_trip = math.gcd(nk, ATTN_CHUNKS_PER_TRIP)
    bufs = (s_a, s_b)

    def body(jj, carry):
        mc, m, acc = carry
        j = per_trip * jj
        for c in range(per_trip):
            mc_next = scores(jnp.minimum(j + c + 1, nk - 1), bufs[(c + 1) % 2])
            m, acc = update(j + c, bufs[c % 2], mc, m, acc)
            mc = mc_next
        return mc, m, acc

    m0 = jnp.full((1, 2 * tq), NEG_BIG, F32)
    a0 = jnp.zeros((DIFF_V_DIM + ONES_ROWS, 2 * tq), F32)
    _, _, acc = lax.fori_loop(0, nk // per_trip, body, (scores(0, s_a), m0, a0))
    lam = (jnp.exp(jnp.sum(lq1_ref[...] * lk1_ref[...], axis=1, keepdims=True))
           - jnp.exp(jnp.sum(lq2_ref[...] * lk2_ref[...], axis=1, keepdims=True)) + lam_init)
    on = acc[:DIFF_V_DIM] / acc[DIFF_V_DIM:DIFF_V_DIM + 1]
    ot = on[:, :tq] - lam * on[:, tq:]
    ot = ot * lax.rsqrt(jnp.mean(ot * ot, axis=0, keepdims=True) + LN_EPS)
    o_ref[0] = (ot.T * g_ref[...] * (1.0 - lam_init)).astype(o_ref.dtype)


def _attn_call(qa, ka, va, subln_g, lq1, lk1, lq2, lk2, lam_init, tq, tk):
    b, l, w = qa.shape
    nk = l // tk
    qt = qa.transpose(0, 2, 1)
    vt = va.reshape(b, nk, tk, w).transpose(0, 1, 3, 2)
    cmap = lambda bi, h, i: (0, 0)
    small = pl.BlockSpec((1, DIFF_HEAD_DIM), cmap)
    return pl.pallas_call(
        functools.partial(_attn_kernel, lam_init=lam_init),
        grid=(b, DIFF_HEADS, l // tq),
        in_specs=[
            pl.BlockSpec((1, LANES, tq), lambda bi, h, i: (bi, h, i)),
            pl.BlockSpec((1, l, LANES), lambda bi, h, i: (bi, 0, h)),
            pl.BlockSpec((1, nk, LANES, tk), lambda bi, h, i: (bi, 0, h, 0)),
            pl.BlockSpec((1, DIFF_V_DIM), cmap),
            small, small, small, small,
        ],
        out_specs=pl.BlockSpec((1, tq, LANES), lambda bi, h, i: (bi, i, h)),
        out_shape=jax.ShapeDtypeStruct((b, l, DIFF_WIDTH), BF16),
        scratch_shapes=[pltpu.VMEM((tk, 2 * tq), F32), pltpu.VMEM((tk, 2 * tq), F32)],
        compiler_params=_cparams("arbitrary", "arbitrary", "arbitrary"),
        name="attn",
    )(qt, ka, vt, subln_g, lq1, lk1, lq2, lk2)


def _s5_kernel(u_ref, bf_ref, bb_ref, c_ref, a_ref, y_ref, xs_ref, st_ref, *, tc, rb):
    n = S5_NS
    rows = tc * SUBLANES

    @pl.when(pl.program_id(0) == 0)
    def _():
        st_ref[...] = jnp.zeros_like(st_ref)

    def in_proj(i, c):
        r0 = pl.multiple_of(i * rb, rb)
        u = u_ref[pl.ds(r0, rb), :]
        fwd = (lax.broadcasted_iota(jnp.int32, u.shape, 0) % SUBLANES) < SUBLANES // 2
        zero = jnp.zeros_like(u)
        xs_ref[pl.ds(r0, rb), :] = (
            jnp.dot(jnp.where(fwd, u, zero), bf_ref[...], preferred_element_type=F32)
            + jnp.dot(jnp.where(fwd, zero, u), bb_ref[...], preferred_element_type=F32))
        return c

    lax.fori_loop(0, rows // rb, in_proj, 0)

    a_re = a_ref[:, :n]
    a_im = a_ref[:, n:]

    def step(t, carry):
        h_re, h_im = carry
        r0 = pl.multiple_of(t * SUBLANES, SUBLANES)
        x_re = xs_ref[pl.ds(r0, SUBLANES), :n]
        x_im = xs_ref[pl.ds(r0, SUBLANES), n:]
        n_re = a_re * h_re - a_im * h_im + x_re
        n_im = a_re * h_im + a_im * h_re + x_im
        xs_ref[pl.ds(r0, SUBLANES), :n] = n_re
        xs_ref[pl.ds(r0, SUBLANES), n:] = n_im
        return n_re, n_im

    h_re, h_im = lax.fori_loop(0, tc, step, (st_ref[:, :n], st_ref[:, n:]), unroll=4)
    st_ref[:, :n] = h_re
    st_ref[:, n:] = h_im

    def out_proj(i, c):
        r0 = pl.multiple_of(i * rb, rb)
        hs = xs_ref[pl.ds(r0, rb), :].astype(BF16)
        yy = jnp.dot(hs, c_ref[...], preferred_element_type=F32)
        fwd = (lax.broadcasted_iota(jnp.int32, (rb, S5_WIDTH), 0) % SUBLANES) < SUBLANES // 2
        y_ref[pl.ds(r0, rb), :] = jnp.where(fwd, yy[:, :S5_WIDTH], yy[:, S5_WIDTH:])
        return c

    lax.fori_loop(0, rows // rb, out_proj, 0)


def _s5_call(u_int, bmat_f, bmat_b, cmat, a_rows, tc):
    rows_total = u_int.shape[0]
    rows = tc * SUBLANES
    rb = min(rows, 256)
    full = lambda i: (0, 0)
    return pl.pallas_call(
        functools.partial(_s5_kernel, tc=tc, rb=rb),
        grid=(rows_total // rows,),
        in_specs=[
            pl.BlockSpec((rows, S5_WIDTH), lambda i: (i, 0)),
            pl.BlockSpec(bmat_f.shape, full),
            pl.BlockSpec(bmat_b.shape, full),
            pl.BlockSpec(cmat.shape, full),
            pl.BlockSpec(a_rows.shape, full),
        ],
        out_specs=pl.BlockSpec((rows, S5_WIDTH), lambda i: (i, 0)),
        out_shape=jax.ShapeDtypeStruct((rows_total, S5_WIDTH), F32),
        scratch_shapes=[pltpu.VMEM((rows, 2 * S5_NS), F32), pltpu.VMEM((SUBLANES, 2 * S5_NS), F32)],
        compiler_params=_cparams("arbitrary"),
        name="s5",
    )(u_int, bmat_f, bmat_b, cmat, a_rows)


def _s5_params(lam_re, lam_im, log_step, b_re, b_im, c_re, c_im, batch):
    lr = jnp.minimum(lam_re.astype(F32), -1e-4)
    li = lam_im.astype(F32)
    dt = jnp.exp(log_step.astype(F32))[..., None]
    mag = jnp.exp(lr * dt)
    ab_re = mag * jnp.cos(li * dt)
    ab_im = mag * jnp.sin(li * dt)
    den = lr * lr + li * li
    f_re = ((ab_re - 1.0) * lr + ab_im * li) / den
    f_im = (ab_im * lr - (ab_re - 1.0) * li) / den
    bre = b_re.astype(F32)
    bim = b_im.astype(F32)
    bb_re = f_re[..., None] * bre - f_im[..., None] * bim
    bb_im = f_re[..., None] * bim + f_im[..., None] * bre
    eye = jnp.eye(S5_GROUPS, dtype=F32)

    def in_mat(m):
        return jnp.einsum('gpn,gh->gnhp', m, eye).reshape(S5_WIDTH, S5_NS)

    def out_mat(m):
        return jnp.einsum('gnp,gh->gphn', m, eye).reshape(S5_NS, S5_WIDTH)

    bmats = [jnp.concatenate([in_mat(bb_re[d]), in_mat(bb_im[d])], axis=1).astype(BF16) for d in range(2)]
    cmat = jnp.concatenate(
        [jnp.concatenate([out_mat(c_re[d].astype(F32)), -out_mat(c_im[d].astype(F32))], axis=0) for d in range(2)],
        axis=1).astype(BF16)
    a_dir = jnp.concatenate([ab_re.reshape(2, S5_NS), ab_im.reshape(2, S5_NS)], axis=1)
    a_rows = jnp.repeat(a_dir, batch, axis=0)
    return bmats[0], bmats[1], cmat, a_rows


def _ret_kernel(q_ref, k_ref, v_ref, g_ref, dm_ref, wf_ref, wb_ref, gf_ref, gb_ref, dec_ref,
                o_ref, y_ref, *, chunk):
    seq = q_ref.shape[1]
    n = seq // chunk
    lane = lax.broadcasted_iota(jnp.int32, (chunk, LANES), 1)
    head_lanes = [lane < RET_HEAD_DIM, lane >= RET_HEAD_DIM]
    nt = (((1,), (1,)), ((), ()))
    tn = (((0,), (0,)), ((), ()))

    def load(ref, c0):
        return ref[0, pl.ds(c0, chunk), :]

    def fwd(i, states):
        c0 = pl.multiple_of(i * chunk, chunk)
        qc, kc, vc = load(q_ref, c0), load(k_ref, c0), load(v_ref, c0)
        y = jnp.zeros((chunk, LANES), F32)
        new_states = []
        for h in range(2):
            qm = jnp.where(head_lanes[h], qc, jnp.zeros_like(qc))
            s = lax.dot_general(qm, kc, nt, preferred_element_type=F32) * dm_ref[h]
            yh = jnp.dot(s.astype(BF16), vc, preferred_element_type=F32)
            yh = yh + jnp.dot(qm, states[h].astype(BF16), preferred_element_type=F32) * gf_ref[h]
            kw = (kc.astype(F32) * wf_ref[h]).astype(BF16)
            new_states.append(dec_ref[h, 0:1, :] * states[h]
                              + lax.dot_general(kw, vc, tn, preferred_element_type=F32))
            y = jnp.where(head_lanes[h], yh, y)
        y_ref[pl.ds(c0, chunk), :] = y
        return tuple(new_states)

    z = jnp.zeros((LANES, LANES), F32)
    unroll = math.gcd(n, 4)
    lax.fori_loop(0, n, fwd, (z, z), unroll=unroll)

    def bwd(ii, states):
        i = n - 1 - ii
        c0 = pl.multiple_of(i * chunk, chunk)
        qc, kc, vc = load(q_ref, c0), load(k_ref, c0), load(v_ref, c0)
        y = y_ref[pl.ds(c0, chunk), :]
        new_states = []
        for h in range(2):
            qm = jnp.where(head_lanes[h], qc, jnp.zeros_like(qc))
            yh = jnp.dot(qm, states[h].astype(BF16), preferred_element_type=F32) * gb_ref[h]
            kw = (kc.astype(F32) * wb_ref[h]).astype(BF16)
            new_states.append(dec_ref[h, 1:2, :] * states[h]
                              + lax.dot_general(kw, vc, tn, preferred_element_type=F32))
            y = y + jnp.where(head_lanes[h], yh, 0.0)
        inv = 1.0 / RET_HEAD_DIM
        mu = [jnp.sum(jnp.where(m, y, 0.0), axis=1, keepdims=True) * inv for m in head_lanes]
        yc = y - jnp.where(head_lanes[0], mu[0], mu[1])
        var = [jnp.sum(jnp.where(m, yc * yc, 0.0), axis=1, keepdims=True) * inv for m in head_lanes]
        yn = yc * lax.rsqrt(jnp.where(head_lanes[0], var[0], var[1]) + LN_EPS)
        g = load(g_ref, c0).astype(F32)
        o_ref[0, pl.ds(c0, chunk), :] = (g * jax.nn.sigmoid(g) * yn).astype(o_ref.dtype)
        return tuple(new_states)

    lax.fori_loop(0, n, bwd, (z, z), unroll=unroll)


def _ret_tables(chunk):
    hidx = jnp.arange(RET_HEADS, dtype=F32)
    log_gf = jnp.log1p(-jnp.exp2(-5.0 - hidx))
    log_gb = jnp.log1p(-jnp.exp2(-5.5 - hidx))
    j = jnp.arange(chunk, dtype=F32)
    rel = j[:, None] - j[None, :]
    dmask = jnp.exp(jnp.where(rel >= 0, log_gf[:, None, None], log_gb[:, None, None]) * jnp.abs(rel))
    rows = lambda t: jnp.broadcast_to(t[:, :, None], (RET_HEADS, chunk, LANES))
    wf = rows(jnp.exp(log_gf[:, None] * (chunk - 1.0 - j)))
    wb = rows(jnp.exp(log_gb[:, None] * j))
    gf = rows(jnp.exp(log_gf[:, None] * (j + 1.0)))
    gb = rows(jnp.exp(log_gb[:, None] * (chunk - j)))
    dec = jnp.stack([jnp.exp(log_gf * chunk), jnp.exp(log_gb * chunk)], axis=1)
    dec = jnp.broadcast_to(jnp.pad(dec, ((0, 0), (0, SUBLANES - 2)))[:, :, None], (RET_HEADS, SUBLANES, LANES))
    return dmask, wf, wb, gf, gb, dec


def _ret_call(qr, kr, vr, gr, tables, chunk):
    b, l, _ = qr.shape
    dmask, wf, wb, gf, gb, dec = tables
    seq = lambda bi, p: (bi, 0, p)
    pair = lambda bi, p: (p, 0, 0)
    seq_spec = pl.BlockSpec((1, l, LANES), seq)
    row_spec = pl.BlockSpec((2, chunk, LANES), pair)
    return pl.pallas_call(
        functools.partial(_ret_kernel, chunk=chunk),
        grid=(b, RET_HEADS // 2),
        in_specs=[seq_spec, seq_spec, seq_spec, seq_spec,
                  pl.BlockSpec((2, chunk, chunk), pair),
                  row_spec, row_spec, row_spec, row_spec,
                  pl.BlockSpec((2, SUBLANES, LANES), pair)],
        out_specs=seq_spec,
        out_shape=jax.ShapeDtypeStruct((b, l, RET_WIDTH), BF16),
        scratch_shapes=[pltpu.VMEM((l, LANES), F32)],
        compiler_params=_cparams("arbitrary", "arbitrary"),
        name="ret",
    )(qr, kr, vr, gr, dmask, wf, wb, gf, gb, dec)


def _post_kernel(ya_ref, yf_ref, yb_ref, ub_ref, yc_ref, x_ref, dsk_ref, gw_ref, gb_ref,
                 woa_ref, wob_ref, woc_ref, g1_ref, lng_ref, lnb_ref, sc2_ref, sh2_ref, rwt_ref, rb_ref,
                 x1_ref, h2_ref, idx_ref, gate_ref, *, alpha):
    ybw = yb_ref[0]
    hi = ybw.astype(BF16)
    lo = (ybw - hi.astype(F32)).astype(BF16)
    rev = _anti_identity(ybw.shape[0])
    ybw = jnp.dot(rev, hi, preferred_element_type=F32) + jnp.dot(rev, lo, preferred_element_type=F32)
    y = yf_ref[0] + ybw + dsk_ref[...] * ub_ref[0]
    y = jax.nn.gelu(y).astype(BF16)
    z = jnp.dot(y, gw_ref[...], preferred_element_type=F32) + gb_ref[...]
    yb = (z[:, :S5_WIDTH] * jax.nn.sigmoid(z[:, S5_WIDTH:])).astype(BF16)
    mix = (jnp.dot(ya_ref[0], woa_ref[...], preferred_element_type=F32)
           + jnp.dot(yb, wob_ref[...], preferred_element_type=F32)
           + jnp.dot(yc_ref[0], woc_ref[...], preferred_element_type=F32))
    x1 = _ln(alpha * x_ref[0] + (1.0 + g1_ref[0]) * mix) * lng_ref[...] + lnb_ref[...]
    x1_ref[0] = x1
    h2 = _ln(x1) * (1.0 + sc2_ref[0]) + sh2_ref[0]
    _store_rows_as_tiles(h2_ref, h2)
    logits = lax.dot_general(rwt_ref[...], h2, (((1,), (1,)), ((), ())),
                             preferred_element_type=F32, precision=HIGHEST) + rb_ref[...]
    eidx = lax.broadcasted_iota(jnp.int32, logits.shape, 0)
    vals, idxs = [], []
    cur = logits
    for _ in range(TOP_K):
        m = jnp.max(cur, axis=0, keepdims=True)
        i = jnp.min(jnp.where(cur == m, eidx, N_EXPERTS), axis=0, keepdims=True)
        vals.append(m)
        idxs.append(i)
        cur = jnp.where(eidx == i, -jnp.inf, cur)
    tv = jnp.concatenate(vals, axis=0)
    e = jnp.exp(tv - vals[0])
    gate_ref[...] = e / jnp.sum(e, axis=0, keepdims=True)
    idx_ref[...] = jnp.concatenate(idxs, axis=0)


def _post_call(ya, yf, yb, ub, yc, x, dsk, gw, gb, woa, wob, woc, g1, lng, lnb, sc2, sh2, rwt, rb, alpha, tm):
    b, l, d = x.shape
    row = lambda bi, i: (bi, i, 0)
    vec = lambda bi, i: (bi, 0, 0)
    full2 = lambda bi, i: (0, 0)
    nl = l // tm
    tok = lambda bi, i: (0, bi * nl + i)

    def fs(a):
        return pl.BlockSpec(a.shape, full2)

    return pl.pallas_call(
        functools.partial(_post_kernel, alpha=alpha),
        grid=(b, nl),
        in_specs=[
            pl.BlockSpec((1, tm, DIFF_WIDTH), row),
            pl.BlockSpec((1, tm, S5_WIDTH), row),
            pl.BlockSpec((1, tm, S5_WIDTH), lambda bi, i: (bi, nl - 1 - i, 0)),
            pl.BlockSpec((1, tm, S5_WIDTH), row),
            pl.BlockSpec((1, tm, RET_WIDTH), row),
            pl.BlockSpec((1, tm, d), row),
            fs(dsk), fs(gw), fs(gb), fs(woa), fs(wob), fs(woc),
            pl.BlockSpec((1, 1, d), vec),
            fs(lng), fs(lnb),
            pl.BlockSpec((1, 1, d), vec),
            pl.BlockSpec((1, 1, d), vec),
            fs(rwt), fs(rb),
        ],
        out_specs=[
            pl.BlockSpec((1, tm, d), row),
            pl.BlockSpec((tm * TOK_TILE, LANES), lambda bi, i: (bi * nl + i, 0)),
            pl.BlockSpec((TOP_K, tm), tok),
            pl.BlockSpec((TOP_K, tm), tok),
        ],
        out_shape=[
            jax.ShapeDtypeStruct((b, l, d), F32),
            jax.ShapeDtypeStruct((b * l * TOK_TILE, LANES), F32),
            jax.ShapeDtypeStruct((TOP_K, b * l), jnp.int32),
            jax.ShapeDtypeStruct((TOP_K, b * l), F32),
        ],
        compiler_params=_cparams("arbitrary", "arbitrary"),
        name="post",
    )(ya, yf, yb, ub, yc, x, dsk, gw, gb, woa, wob, woc, g1, lng, lnb, sc2, sh2, rwt, rb)


TOK_TILE = D_MODEL // LANES


def _tiles_to_rows(ref, lead, first, n):
    return jnp.concatenate(
        [ref[lead, pl.ds(first * TOK_TILE + s, n, stride=TOK_TILE), :] for s in range(TOK_TILE)], axis=1)


def _store_rows_as_tiles(ref, x):
    n = x.shape[0]
    for s in range(TOK_TILE):
        ref[pl.ds(s, n, stride=TOK_TILE), :] = x[:, s * LANES:(s + 1) * LANES]


GATHER_GROUP = 32


def _gather_start(src_hbm, row_ref, n, dst, slot, sem, dst_token):
    def group(g, c):
        for k in range(GATHER_GROUP):
            r = g * GATHER_GROUP + k
            src = pl.multiple_of(row_ref[0, 0, r], TOK_TILE)
            row = pl.multiple_of(dst_token(r) * TOK_TILE, TOK_TILE)
            pltpu.make_async_copy(src_hbm.at[pl.ds(src, TOK_TILE), :],
                                  dst.at[slot, pl.ds(row, TOK_TILE), :], sem.at[slot]).start(priority=k % 2)
        return c

    lax.fori_loop(0, n // GATHER_GROUP, group, 0)


def _gather_wait(dst, slot, sem):
    pltpu.make_async_copy(dst.at[1 - slot], dst.at[slot], sem.at[slot]).wait()


def _expert_kernel(be_ref, nu_ref, row_ref, rown_ref, h_hbm, w1_ref, b1_ref, w2_ref, b2_ref, o_ref, buf, sem):
    eb = row_ref.shape[2]
    ff = w2_ref.shape[1]
    i = pl.program_id(0)
    last = pl.num_programs(0) - 1
    slot = i % 2

    @pl.when(i == 0)
    def _():
        _gather_start(h_hbm, row_ref, eb, buf, 0, sem, lambda r: r)

    _gather_wait(buf, slot, sem)

    def prefetch():
        _gather_start(h_hbm, rown_ref, eb, buf, 1 - slot, sem, lambda r: r)

    @pl.when(i < nu_ref[0])
    def _():
        prefetch()
        hb = _tiles_to_rows(buf, slot, 0, eb).astype(BF16)
        z = jnp.dot(hb, w1_ref[0], preferred_element_type=F32) + b1_ref[0]
        acts = []
        for j in range(ff // LANES):
            z_glu = jnp.minimum(z[:, 2 * j * LANES:(2 * j + 1) * LANES], SWIGLU_LIMIT)
            z_lin = jnp.clip(z[:, (2 * j + 1) * LANES:(2 * j + 2) * LANES], -SWIGLU_LIMIT, SWIGLU_LIMIT)
            acts.append(z_glu * jax.nn.sigmoid(SWIGLU_ALPHA * z_glu) * (z_lin + 1.0))
        act = jnp.concatenate(acts, axis=1).astype(BF16)
        _store_rows_as_tiles(o_ref, jnp.dot(act, w2_ref[0], preferred_element_type=F32) + b2_ref[0])

    @pl.when(i >= nu_ref[0])
    def _():
        prefetch()
        o_ref[...] = jnp.zeros_like(o_ref)

    @pl.when(i == last)
    def _():
        _gather_wait(buf, 1 - slot, sem)


def _expert_call(block_e, n_used, slot_row, h2_tiles, w1, b1, w2, b2, layer):
    n_blocks = block_e.shape[0]
    wmap = lambda i, be, nu: (layer * N_EXPERTS + be[i], 0, 0)
    d = w1.shape[1]
    ff = w2.shape[1]
    eb = EXPERT_BLOCK
    rows3 = slot_row.reshape(n_blocks, 1, eb)
    grid_spec = pltpu.PrefetchScalarGridSpec(
        num_scalar_prefetch=2,
        grid=(n_blocks,),
        in_specs=[
            pl.BlockSpec((1, 1, eb), lambda i, be, nu: (i, 0, 0), memory_space=pltpu.SMEM),
            pl.BlockSpec((1, 1, eb), lambda i, be, nu: (jnp.minimum(i + 1, n_blocks - 1), 0, 0),
                         memory_space=pltpu.SMEM),
            pl.BlockSpec(memory_space=pl.ANY),
            pl.BlockSpec((1, d, 2 * ff), wmap),
            pl.BlockSpec((1, 1, 2 * ff), wmap),
            pl.BlockSpec((1, ff, d), wmap),
            pl.BlockSpec((1, 1, d), wmap),
        ],
        out_specs=pl.BlockSpec((eb * TOK_TILE, LANES), lambda i, be, nu: (i, 0)),
        scratch_shapes=[pltpu.VMEM((2, eb * TOK_TILE, LANES), F32), pltpu.SemaphoreType.DMA((2,))],
    )
    return pl.pallas_call(
        _expert_kernel,
        grid_spec=grid_spec,
        out_shape=jax.ShapeDtypeStruct((n_blocks * eb * TOK_TILE, LANES), F32),
        compiler_params=_cparams("arbitrary"),
        name="expert",
    )(block_e, n_used, rows3, rows3, h2_tiles, w1, b1, w2, b2)


def _w1prep_kernel(w_ref, p_ref, o_ref):
    for j in range(w_ref.shape[1] // (2 * LANES)):
        cols = slice(2 * j * LANES, 2 * (j + 1) * LANES)
        o_ref[:, cols] = jnp.dot(w_ref[:, cols].astype(BF16), p_ref[...], preferred_element_type=F32).astype(BF16)


def _w1prep_call(w1_rows, tr):
    n, c = w1_rows.shape
    src = jnp.arange(2 * LANES, dtype=jnp.int32)
    src = jnp.where(src < LANES, 2 * src, 2 * (src - LANES) + 1)
    perm = (jnp.arange(2 * LANES, dtype=jnp.int32)[:, None] == src[None, :]).astype(BF16)
    return pl.pallas_call(
        _w1prep_kernel,
        grid=(n // tr,),
        in_specs=[pl.BlockSpec((tr, c), lambda i: (i, 0)), pl.BlockSpec(perm.shape, lambda i: (0, 0))],
        out_specs=pl.BlockSpec((tr, c), lambda i: (i, 0)),
        out_shape=jax.ShapeDtypeStruct((n, c), BF16),
        compiler_params=_cparams("arbitrary"),
        name="w1prep",
    )(w1_rows, perm)


def _combine_kernel(row_ref, rown_ref, ys_hbm, gate_ref, x1_ref, g2_ref, lng_ref, lnb_ref, o_ref, buf, sem, *, alpha):
    tm = x1_ref.shape[0]
    i = pl.program_id(0)
    slot = i % 2
    place = lambda r: r

    @pl.when(i == 0)
    def _():
        _gather_start(ys_hbm, row_ref, tm * TOP_K, buf, 0, sem, place)

    _gather_wait(buf, slot, sem)
    _gather_start(ys_hbm, rown_ref, tm * TOP_K, buf, 1 - slot, sem, place)
    gates = gate_ref[...]
    ffo = _tiles_to_rows(buf, slot, 0, tm) * gates[:, 0:1]
    for k in range(1, TOP_K):
        ffo = ffo + _tiles_to_rows(buf, slot, k * tm, tm) * gates[:, k:k + 1]
    o_ref[...] =_ln(alpha * x1_ref[...] + (1.0 + g2_ref[0]) * ffo) * lng_ref[...] + lnb_ref[...]

    @pl.when(i == pl.num_programs(0) - 1)
    def _():
        _gather_wait(buf, 1 - slot, sem)


def _combine_call(dest_row, ys_tiles, gates_t, x1, g2, lng, lnb, alpha, tm):
    b, l, d = x1.shape
    nl = l // tm
    n_tiles = b * nl
    rows3 = dest_row.reshape(n_tiles, tm, TOP_K).transpose(0, 2, 1).reshape(n_tiles, 1, tm * TOP_K)
    full2 = lambda i: (0, 0)
    return pl.pallas_call(
        functools.partial(_combine_kernel, alpha=alpha),
        grid=(n_tiles,),
        in_specs=[
            pl.BlockSpec((1, 1, tm * TOP_K), lambda i: (i, 0, 0), memory_space=pltpu.SMEM),
            pl.BlockSpec((1, 1, tm * TOP_K), lambda i: (jnp.minimum(i + 1, n_tiles - 1), 0, 0),
                         memory_space=pltpu.SMEM),
            pl.BlockSpec(memory_space=pl.ANY),
            pl.BlockSpec((tm, TOP_K), lambda i: (i, 0)),
            pl.BlockSpec((tm, d), lambda i: (i, 0)),
            pl.BlockSpec((1, 1, d), lambda i: (i // nl, 0, 0)),
            pl.BlockSpec(lng.shape, full2),
            pl.BlockSpec(lnb.shape, full2),
        ],
        out_specs=pl.BlockSpec((tm, d), lambda i: (i, 0)),
        out_shape=jax.ShapeDtypeStruct((b * l, d), F32),
        scratch_shapes=[pltpu.VMEM((2, TOP_K * tm * TOK_TILE, LANES), F32), pltpu.SemaphoreType.DMA((2,))],
        compiler_params=_cparams("arbitrary"),
        name="combine",
    )(rows3, rows3, ys_tiles, gates_t, x1.reshape(b * l, d), g2, lng, lnb).reshape(b, l, d)


def _route(top_idx, n_tok):
    n_assign = n_tok * TOP_K
    n_blocks = n_assign // EXPERT_BLOCK + N_EXPERTS
    e_flat = top_idx.T.reshape(-1)
    order = jnp.argsort(e_flat)
    e_sorted = e_flat[order]
    counts = jnp.sum((e_flat[:, None] == jnp.arange(N_EXPERTS, dtype=jnp.int32)[None, :]).astype(jnp.int32), axis=0)
    start = jnp.cumsum(counts) - counts
    padded = (counts + EXPERT_BLOCK - 1) // EXPERT_BLOCK * EXPERT_BLOCK
    pend = jnp.cumsum(padded)
    pstart = pend - padded
    dest_sorted = pstart[e_sorted] + jnp.arange(n_assign, dtype=jnp.int32) - start[e_sorted]
    _, dest = lax.sort((order.astype(jnp.int32), dest_sorted.astype(jnp.int32)), num_keys=1)
    block_start = jnp.arange(n_blocks, dtype=jnp.int32) * EXPERT_BLOCK
    block_e = jnp.sum((pend[None, :] <= block_start[:, None]).astype(jnp.int32), axis=1)
    block_e = jnp.minimum(block_e, N_EXPERTS - 1)
    slot = jnp.arange(n_blocks * EXPERT_BLOCK, dtype=jnp.int32)
    slot_e = jnp.repeat(block_e, EXPERT_BLOCK)
    within = slot - pstart[slot_e]
    valid = (within < counts[slot_e]) & (slot < pend[-1])
    src = jnp.clip(start[slot_e] + within, 0, n_assign - 1)
    slot_tok = jnp.where(valid, (order[src] // TOP_K).astype(jnp.int32), 0)
    n_used = (pend[-1] // EXPERT_BLOCK).astype(jnp.int32).reshape(1)
    return block_e, n_used, slot_tok, dest


def _pick(n, pref):
    return pref if n % pref == 0 else n


def kernel(x, c, positions, ada_w, ada_b, w_in, w_out, diff_lambda_q1, diff_lambda_k1, diff_lambda_q2,
           diff_lambda_k2, diff_subln_g, s5_lambda_re, s5_lambda_im, s5_log_step, s5_b_re, s5_b_im, s5_c_re,
           s5_c_im, s5_d, s5_glu_w, s5_glu_b, ln_g, ln_b, router_w, router_b, exp_w1, exp_b1, exp_w2, exp_b2):
    bsz, seq, d = x.shape
    depth = ada_w.shape[0]
    n_tok = bsz * seq
    alpha = (2 * depth) ** 0.25

    def tables(inv_freq):
        ang = positions.astype(F32)[..., None] * inv_freq
        cs, sn = jnp.cos(ang), jnp.sin(ang)
        return jnp.concatenate([cs, cs, cs, cs], axis=-1), jnp.concatenate([-sn, sn, -sn, sn], axis=-1)

    cos_a, sin_a = tables(1.0 / (ROPE_THETA ** (jnp.arange(0, DIFF_HEAD_DIM, 2, dtype=F32) / DIFF_HEAD_DIM)))
    cos_r, sin_r = tables(1.0 / (ROPE_THETA ** jnp.linspace(0.0, 1.0, RET_HEAD_DIM // 2, dtype=F32)))
    ret_tables = _ret_tables(RET_CHUNK)

    c_pad = jnp.pad(c.astype(F32), ((0, SUBLANES - bsz), (0, 0)))
    mod = _ada_call(c_pad, ada_w, ada_b)[:, :bsz]

    tm_pre = _pick(seq, 512)
    tq = _pick(seq, 256)
    tk = 512 if seq % 1024 == 0 else seq // 2
    tc = _pick(seq, 128)
    tm_post = _pick(seq, 512) if seq > 512 else seq // 2
    tm_comb = _pick(seq, 128)

    n_e, two_ff = exp_w1.shape[1], exp_w1.shape[3]
    w1 = _w1prep_call(exp_w1.reshape(depth * n_e * d, two_ff), 512).reshape(depth * n_e, d, two_ff)
    b1 = exp_b1.reshape(depth * n_e, two_ff // (2 * LANES), LANES, 2).transpose(0, 1, 3, 2)
    b1 = b1.reshape(depth * n_e, 1, two_ff).astype(F32)
    w2 = exp_w2.astype(BF16).reshape(depth * n_e, exp_w2.shape[2], d)
    b2 = exp_b2.reshape(depth * n_e, 1, d).astype(F32)

    for l in range(depth):
        sh1, sc1, g1, sh2, sc2, g2 = [mod[l, :, None, i * d:(i + 1) * d] for i in range(6)]
        qa, ka, va, ub, ubr, qr, kr, vr, gr = _pre_call(
            x, sc1, sh1, w_in[l].astype(BF16), cos_a, sin_a, cos_r, sin_r, tm_pre)

        lam_init = 0.8 - 0.6 * math.exp(-0.3 * l)
        ya = _attn_call(qa, ka, va, diff_subln_g[l][None, :].astype(F32),
                        diff_lambda_q1[l][None, :], diff_lambda_k1[l][None, :],
                        diff_lambda_q2[l][None, :], diff_lambda_k2[l][None, :], lam_init, tq, tk)

        u_seq = jnp.concatenate([ub.astype(BF16), ubr], axis=0)
        u_int = u_seq.transpose(1, 0, 2).reshape(seq * 2 * bsz, S5_WIDTH)
        bmat_f, bmat_b, cmat, a_rows = _s5_params(
            s5_lambda_re[l], s5_lambda_im[l], s5_log_step[l], s5_b_re[l], s5_b_im[l], s5_c_re[l], s5_c_im[l], bsz)
        y_int = _s5_call(u_int, bmat_f, bmat_b, cmat, a_rows, tc).reshape(seq, 2 * bsz, S5_WIDTH)
        y_fwd = y_int[:, :bsz].transpose(1, 0, 2)
        y_bwd = y_int[:, bsz:].transpose(1, 0, 2)

        yc = _ret_call(qr, kr, vr, gr, ret_tables, RET_CHUNK)

        wo = w_out[l].astype(BF16)
        x1, h2, top_idx, gates = _post_call(
            ya, y_fwd, y_bwd, ub, yc, x,
            s5_d[l][None, :].astype(F32), s5_glu_w[l].astype(BF16), s5_glu_b[l][None, :].astype(F32),
            wo[:DIFF_WIDTH], wo[DIFF_WIDTH:DIFF_WIDTH + S5_WIDTH], wo[DIFF_WIDTH + S5_WIDTH:],
            g1, ln_g[l, 0][None, :], ln_b[l, 0][None, :], sc2, sh2,
            router_w[l].T.astype(F32), router_b[l][:, None].astype(F32), alpha, tm_post)

        block_e, n_used, slot_tok, dest = _route(top_idx, n_tok)
        ys = _expert_call(block_e, n_used, slot_tok * TOK_TILE, h2, w1, b1, w2, b2, l)
        x = _combine_call(dest * TOK_TILE, ys, gates.T, x1, g2, ln_g[l, 1][None, :], ln_b[l, 1][None, :],
                          alpha, tm_comb)
    return x
```

```python
import functools
import math

import jax
import jax.numpy as jnp
from jax import lax
from jax.experimental import pallas as pl
from jax.experimental.pallas import tpu as pltpu

F32 = jnp.float32
BF16 = jnp.bfloat16
HIGHEST = lax.Precision.HIGHEST

LANES = 128
SUBLANES = 8
VMEM_LIMIT = 56 * 1024 * 1024

D_MODEL = 1024
DIFF_HEADS = 4
DIFF_WIDTH = D_MODEL // 2
DIFF_V_DIM = DIFF_WIDTH // DIFF_HEADS
DIFF_HEAD_DIM = DIFF_V_DIM // 2
DIFF_QK = DIFF_HEADS * 2 * DIFF_HEAD_DIM
ROPE_THETA = 10000.0
S5_WIDTH = D_MODEL // 4
S5_GROUP = 16
S5_GROUPS = S5_WIDTH // S5_GROUP
S5_STATE = 64
S5_NS = S5_GROUPS * S5_STATE
RET_HEADS = 4
RET_WIDTH = D_MODEL // 4
RET_HEAD_DIM = RET_WIDTH // RET_HEADS
RET_CHUNK = 128
IN_WIDTH = 2 * DIFF_QK + DIFF_WIDTH + S5_WIDTH + 4 * RET_WIDTH
N_EXPERTS = 32
TOP_K = 4
EXPERT_FF = D_MODEL
SWIGLU_ALPHA = 1.702
SWIGLU_LIMIT = 7.0
EXPERT_BLOCK = 256
LN_EPS = 1e-5
NEG_BIG = -1e30

OFF_QA = 0
OFF_KA = OFF_QA + DIFF_QK
OFF_VA = OFF_KA + DIFF_QK
OFF_UB = OFF_VA + DIFF_WIDTH
OFF_QR = OFF_UB + S5_WIDTH
OFF_KR = OFF_QR + RET_WIDTH
OFF_VR = OFF_KR + RET_WIDTH
OFF_GR = OFF_VR + RET_WIDTH


def _cparams(*sem):
    return pltpu.CompilerParams(dimension_semantics=sem, vmem_limit_bytes=VMEM_LIMIT)


def _ln(x):
    mu = jnp.mean(x, axis=-1, keepdims=True)
    xc = x - mu
    var = jnp.mean(xc * xc, axis=-1, keepdims=True)
    return xc * lax.rsqrt(var + LN_EPS)


def _ada_kernel(c_ref, w_ref, b_ref, o_ref):
    c = c_ref[...]
    cond = c * jax.nn.sigmoid(c)
    o_ref[0] = jnp.dot(cond, w_ref[0], preferred_element_type=F32, precision=HIGHEST) + b_ref[0]


def _ada_call(c_pad, ada_w, ada_b):
    depth, d, n = ada_w.shape
    tn = n // 6
    return pl.pallas_call(
        _ada_kernel,
        grid=(depth, n // tn),
        in_specs=[
            pl.BlockSpec((SUBLANES, d), lambda l, j: (0, 0)),
            pl.BlockSpec((1, d, tn), lambda l, j: (l, 0, j)),
            pl.BlockSpec((1, 1, tn), lambda l, j: (l, 0, j)),
        ],
        out_specs=pl.BlockSpec((1, SUBLANES, tn), lambda l, j: (l, 0, j)),
        out_shape=jax.ShapeDtypeStruct((depth, SUBLANES, n), F32),
        compiler_params=_cparams("arbitrary", "arbitrary"),
        name="ada",
    )(c_pad, ada_w, ada_b.reshape(depth, 1, n))


def _rope_lanes(z, cos, sin_signed, first_half):
    partner = jnp.where(first_half, pltpu.roll(z, LANES - 32, 1), pltpu.roll(z, 32, 1))
    return z * cos + partner * sin_signed


def _anti_identity(n):
    r = lax.broadcasted_iota(jnp.int32, (n, n), 0)
    c = lax.broadcasted_iota(jnp.int32, (n, n), 1)
    return (r + c == n - 1).astype(BF16)


def _pre_kernel(x_ref, sc_ref, sh_ref, w_ref, ca_ref, sa_ref, cr_ref, sr_ref,
                qa_ref, ka_ref, va_ref, ub_ref, ubr_ref, qr_ref, kr_ref, vr_ref, gr_ref):
    tm = x_ref.shape[1]
    h = (_ln(x_ref[0]) * (1.0 + sc_ref[0]) + sh_ref[0]).astype(BF16)
    lane = lax.broadcasted_iota(jnp.int32, (tm, LANES), 1)
    first_half = (lane % 64) < 32

    def proj(off, width):
        return jnp.dot(h, w_ref[:, off:off + width], preferred_element_type=F32)

    def rope_store(z, cos, sin, scale, out_ref):
        for j in range(z.shape[1] // LANES):
            zj = _rope_lanes(z[:, j * LANES:(j + 1) * LANES], cos, sin, first_half)
            out_ref[0, :, j * LANES:(j + 1) * LANES] = (zj * scale).astype(out_ref.dtype)

    ca, sa, cr, sr = ca_ref[0], sa_ref[0], cr_ref[0], sr_ref[0]
    zq = proj(OFF_QA, DIFF_QK)
    for j in range(DIFF_QK // LANES):
        zj = _rope_lanes(zq[:, j * LANES:(j + 1) * LANES], ca, sa, first_half)
        qa_ref[0, j * LANES:(j + 1) * LANES, :] = (zj * (DIFF_HEAD_DIM ** -0.5 * math.log2(math.e))).T.astype(BF16)
    rope_store(proj(OFF_KA, DIFF_QK), ca, sa, 1.0, ka_ref)
    zv = proj(OFF_VA, DIFF_WIDTH)
    for j in range(DIFF_WIDTH // LANES):
        va_ref[0, 0, j * LANES:(j + 1) * LANES, :] = zv[:, j * LANES:(j + 1) * LANES].T.astype(BF16)
    u = proj(OFF_UB, S5_WIDTH)
    ub_ref[0] = u
    ubr_ref[0] = jnp.dot(_anti_identity(tm), u.astype(BF16), preferred_element_type=F32).astype(BF16)
    rope_store(proj(OFF_QR, RET_WIDTH), cr, sr, 1.0, qr_ref)
    rope_store(proj(OFF_KR, RET_WIDTH), cr, sr, RET_HEAD_DIM ** -0.5, kr_ref)
    vr_ref[0] = proj(OFF_VR, RET_WIDTH).astype(vr_ref.dtype)
    gr_ref[0] = proj(OFF_GR, RET_WIDTH).astype(gr_ref.dtype)


def _pre_call(x, sc, sh, w_in_bf, cos_a, sin_a, cos_r, sin_r, tm):
    b, l, d = x.shape
    row = lambda bi, i: (bi, i, 0)
    vec = lambda bi, i: (bi, 0, 0)
    nl = l // tm
    mirrored = lambda bi, i: (bi, nl - 1 - i, 0)
    widths = (DIFF_QK, DIFF_QK, DIFF_WIDTH, S5_WIDTH, S5_WIDTH, RET_WIDTH, RET_WIDTH, RET_WIDTH, RET_WIDTH)
    dtypes = (BF16, BF16, BF16, F32, BF16, BF16, BF16, BF16, BF16)
    maps = (row, row, row, row, mirrored, row, row, row, row)
    specs = [pl.BlockSpec((1, tm, w), m) for w, m in zip(widths, maps)]
    shapes = [jax.ShapeDtypeStruct((b, l, w), dt) for w, dt in zip(widths, dtypes)]
    specs[0] = pl.BlockSpec((1, DIFF_QK, tm), lambda bi, i: (bi, 0, i))
    shapes[0] = jax.ShapeDtypeStruct((b, DIFF_QK, l), BF16)
    specs[2] = pl.BlockSpec((1, 1, DIFF_WIDTH, tm), lambda bi, i: (bi, i, 0, 0))
    shapes[2] = jax.ShapeDtypeStruct((b, nl, DIFF_WIDTH, tm), BF16)
    return pl.pallas_call(
        _pre_kernel,
        grid=(b, l // tm),
        in_specs=[
            pl.BlockSpec((1, tm, d), row),
            pl.BlockSpec((1, 1, d), vec),
            pl.BlockSpec((1, 1, d), vec),
            pl.BlockSpec((d, IN_WIDTH), lambda bi, i: (0, 0)),
            pl.BlockSpec((1, tm, LANES), row),
            pl.BlockSpec((1, tm, LANES), row),
            pl.BlockSpec((1, tm, LANES), row),
            pl.BlockSpec((1, tm, LANES), row),
        ],
        out_specs=specs,
        out_shape=shapes,
        compiler_params=_cparams("arbitrary", "arbitrary"),
        name="pre",
    )(x, sc, sh, w_in_bf, cos_a, sin_a, cos_r, sin_r)


ATTN_CHUNKS_PER_TRIP = 8
ONES_ROWS = 16


def _attn_kernel(qt_ref, k_ref, vt_ref, g_ref, lq1_ref, lk1_ref, lq2_ref, lk2_ref, o_ref, s_a, s_b, *, lam_init):
    tq = qt_ref.shape[2]
    nk, _, tk = vt_ref.shape[1:]
    qt = qt_ref[0]
    row = lax.broadcasted_iota(jnp.int32, qt.shape, 0)
    zero = jnp.zeros_like(qt)
    qq = jnp.concatenate([jnp.where(row < DIFF_HEAD_DIM, qt, zero), jnp.where(row >= DIFF_HEAD_DIM, qt, zero)],
                         axis=1)
    ones = jnp.ones((ONES_ROWS, tk), BF16)

    def scores(j, s_ref):
        start = pl.multiple_of(j * tk, tk)
        s = jnp.dot(k_ref[0, pl.ds(start, tk), :], qq, preferred_element_type=F32)
        s_ref[...] = s
        return jnp.max(s, axis=0, keepdims=True)

    def update(j, s_ref, mc, m, acc):
        m_new = jnp.maximum(m, mc)
        p = jnp.exp2(s_ref[...] - m_new).astype(BF16)
        vt = jnp.concatenate([vt_ref[0, j], ones], axis=0)
        return m_new, jnp.exp2(m - m_new) * acc + jnp.dot(vt, p, preferred_element_type=F32)

    per_trip = math.gcd(nk, ATTN_CHUNKS_PER_TRIP)
    bufs = (s_a, s_b)

    def body(jj, carry):
        mc, m, acc = carry
        j = per_trip * jj
        for c in range(per_trip):
            mc_next = scores(jnp.minimum(j + c + 1, nk - 1), bufs[(c + 1) % 2])
            m, acc = update(j + c, bufs[c % 2], mc, m, acc)
            mc = mc_next
        return mc, m, acc

    m0 = jnp.full((1, 2 * tq), NEG_BIG, F32)
    a0 = jnp.zeros((DIFF_V_DIM + ONES_ROWS, 2 * tq), F32)
    _, _, acc = lax.fori_loop(0, nk // per_trip, body, (scores(0, s_a), m0, a0))
    lam = (jnp.exp(jnp.sum(lq1_ref[...] * lk1_ref[...], axis=1, keepdims=True))
           - jnp.exp(jnp.sum(lq2_ref[...] * lk2_ref[...], axis=1, keepdims=True)) + lam_init)
    on = acc[:DIFF_V_DIM] / acc[DIFF_V_DIM:DIFF_V_DIM + 1]
    ot = on[:, :tq] - lam * on[:, tq:]
    ot = ot * lax.rsqrt(jnp.mean(ot * ot, axis=0, keepdims=True) + LN_EPS)
    o_ref[0] = (ot.T * g_ref[...] * (1.0 - lam_init)).astype(o_ref.dtype)


def _attn_call(qt, ka, vt, subln_g, lq1, lk1, lq2, lk2, lam_init, tq):
    b, l, _ = ka.shape
    nk, tk = vt.shape[1], vt.shape[3]
    cmap = lambda bi, h, i: (0, 0)
    small = pl.BlockSpec((1, DIFF_HEAD_DIM), cmap)
    return pl.pallas_call(
        functools.partial(_attn_kernel, lam_init=lam_init),
        grid=(b, DIFF_HEADS, l // tq),
        in_specs=[
            pl.BlockSpec((1, LANES, tq), lambda bi, h, i: (bi, h, i)),
            pl.BlockSpec((1, l, LANES), lambda bi, h, i: (bi, 0, h)),
            pl.BlockSpec((1, nk, LANES, tk), lambda bi, h, i: (bi, 0, h, 0)),
            pl.BlockSpec((1, DIFF_V_DIM), cmap),
            small, small, small, small,
        ],
        out_specs=pl.BlockSpec((1, tq, LANES), lambda bi, h, i: (bi, i, h)),
        out_shape=jax.ShapeDtypeStruct((b, l, DIFF_WIDTH), BF16),
        scratch_shapes=[pltpu.VMEM((tk, 2 * tq), F32), pltpu.VMEM((tk, 2 * tq), F32)],
        compiler_params=_cparams("arbitrary", "arbitrary", "arbitrary"),
        name="attn",
    )(qt, ka, vt, subln_g, lq1, lk1, lq2, lk2)


def _s5_kernel(uf_ref, ub_ref, bf_ref, bb_ref, c_ref, a_ref, yf_ref, yb_ref, us_ref, xs_ref, st_ref, *, tc, rb):
    n = S5_NS
    nc = n // LANES
    rows = tc * SUBLANES
    batch = uf_ref.shape[0]

    @pl.when(pl.program_id(0) == 0)
    def _():
        st_ref[...] = jnp.zeros_like(st_ref)

    for j in range(SUBLANES):
        src = uf_ref[j] if j < batch else ub_ref[j - batch].astype(F32)
        for c in range(S5_WIDTH // LANES):
            us_ref[c, pl.ds(j, tc, stride=SUBLANES), :] = src[:, c * LANES:(c + 1) * LANES]

    def in_proj(i, carry):
        r0 = pl.multiple_of(i * rb, rb)
        u = jnp.concatenate([us_ref[c, pl.ds(r0, rb), :] for c in range(S5_WIDTH // LANES)], axis=1).astype(BF16)
        fwd = (lax.broadcasted_iota(jnp.int32, u.shape, 0) % SUBLANES) < batch
        zero = jnp.zeros_like(u)
        x = (jnp.dot(jnp.where(fwd, u, zero), bf_ref[...], preferred_element_type=F32)
             + jnp.dot(jnp.where(fwd, zero, u), bb_ref[...], preferred_element_type=F32))
        for c in range(2 * nc):
            xs_ref[c, pl.ds(r0, rb), :] = x[:, c * LANES:(c + 1) * LANES]
        return carry

    lax.fori_loop(0, rows // rb, in_proj, 0)

    a_re = a_ref[:, :n]
    a_im = a_ref[:, n:]

    def step(t, carry):
        h_re, h_im = carry
        r0 = pl.multiple_of(t * SUBLANES, SUBLANES)
        x_re = jnp.concatenate([xs_ref[c, pl.ds(r0, SUBLANES), :] for c in range(nc)], axis=1)
        x_im = jnp.concatenate([xs_ref[nc + c, pl.ds(r0, SUBLANES), :] for c in range(nc)], axis=1)
        n_re = a_re * h_re - a_im * h_im + x_re
        n_im = a_re * h_im + a_im * h_re + x_im
        for c in range(nc):
            xs_ref[c, pl.ds(r0, SUBLANES), :] = n_re[:, c * LANES:(c + 1) * LANES]
            xs_ref[nc + c, pl.ds(r0, SUBLANES), :] = n_im[:, c * LANES:(c + 1) * LANES]
        return n_re, n_im

    h_re, h_im = lax.fori_loop(0, tc, step, (st_ref[:, :n], st_ref[:, n:]), unroll=4)
    st_ref[:, :n] = h_re
    st_ref[:, n:] = h_im

    for j in range(SUBLANES):
        d, bi = divmod(j, batch)
        hs = jnp.concatenate([xs_ref[c, pl.ds(j, tc, stride=SUBLANES), :] for c in range(2 * nc)], axis=1)
        yj = jnp.dot(hs.astype(BF16), c_ref[:, d * S5_WIDTH:(d + 1) * S5_WIDTH], preferred_element_type=F32)
        if d == 0:
            yf_ref[bi] = yj
        else:
            yb_ref[bi] = yj


def _s5_call(ub, ubr, bmat_f, bmat_b, cmat, a_rows, tc):
    bsz, l, w = ub.shape
    assert 2 * bsz == SUBLANES
    rows = tc * SUBLANES
    rb = min(rows, 256)
    full = lambda i: (0, 0)
    seq_spec = pl.BlockSpec((bsz, tc, w), lambda i: (0, i, 0))
    return pl.pallas_call(
        functools.partial(_s5_kernel, tc=tc, rb=rb),
        grid=(l // tc,),
        in_specs=[
            seq_spec, seq_spec,
            pl.BlockSpec(bmat_f.shape, full),
            pl.BlockSpec(bmat_b.shape, full),
            pl.BlockSpec(cmat.shape, full),
            pl.BlockSpec(a_rows.shape, full),
        ],
        out_specs=[seq_spec, seq_spec],
        out_shape=[jax.ShapeDtypeStruct((bsz, l, w), F32), jax.ShapeDtypeStruct((bsz, l, w), F32)],
        scratch_shapes=[pltpu.VMEM((w // LANES, rows, LANES), F32), pltpu.VMEM((2 * S5_NS // LANES, rows, LANES), F32),
                        pltpu.VMEM((SUBLANES, 2 * S5_NS), F32)],
        compiler_params=_cparams("arbitrary"),
        name="s5",
    )(ub, ubr, bmat_f, bmat_b, cmat, a_rows)


def _s5_params(lam_re, lam_im, log_step, b_re, b_im, c_re, c_im, batch):
    lr = jnp.minimum(lam_re.astype(F32), -1e-4)
    li = lam_im.astype(F32)
    dt = jnp.exp(log_step.astype(F32))[..., None]
    mag = jnp.exp(lr * dt)
    ab_re = mag * jnp.cos(li * dt)
    ab_im = mag * jnp.sin(li * dt)
    den = lr * lr + li * li
    f_re = ((ab_re - 1.0) * lr + ab_im * li) / den
    f_im = (ab_im * lr - (ab_re - 1.0) * li) / den
    bre = b_re.astype(F32)
    bim = b_im.astype(F32)
    bb_re = f_re[..., None] * bre - f_im[..., None] * bim
    bb_im = f_re[..., None] * bim + f_im[..., None] * bre
    eye = jnp.eye(S5_GROUPS, dtype=F32)

    def in_mat(m):
        return jnp.einsum('gpn,gh->gnhp', m, eye).reshape(S5_WIDTH, S5_NS)

    def out_mat(m):
        return jnp.einsum('gnp,gh->gphn', m, eye).reshape(S5_NS, S5_WIDTH)

    bmats = [jnp.concatenate([in_mat(bb_re[d]), in_mat(bb_im[d])], axis=1).astype(BF16) for d in range(2)]
    cmat = jnp.concatenate(
        [jnp.concatenate([out_mat(c_re[d].astype(F32)), -out_mat(c_im[d].astype(F32))], axis=0) for d in range(2)],
        axis=1).astype(BF16)
    a_dir = jnp.concatenate([ab_re.reshape(2, S5_NS), ab_im.reshape(2, S5_NS)], axis=1)
    a_rows = jnp.repeat(a_dir, batch, axis=0)
    return bmats[0], bmats[1], cmat, a_rows


def _ret_kernel(q_ref, k_ref, v_ref, g_ref, dm_ref, wf_ref, wb_ref, gf_ref, gb_ref, dec_ref,
                o_ref, y_ref, *, chunk):
    seq = q_ref.shape[1]
    n = seq // chunk
    lane = lax.broadcasted_iota(jnp.int32, (chunk, LANES), 1)
    head_lanes = [lane < RET_HEAD_DIM, lane >= RET_HEAD_DIM]
    nt = (((1,), (1,)), ((), ()))
    tn = (((0,), (0,)), ((), ()))

    def load(ref, c0):
        return ref[0, pl.ds(c0, chunk), :]

    def fwd(i, states):
        c0 = pl.multiple_of(i * chunk, chunk)
        qc, kc, vc = load(q_ref, c0), load(k_ref, c0), load(v_ref, c0)
        y = jnp.zeros((chunk, LANES), F32)
        new_states = []
        for h in range(2):
            qm = jnp.where(head_lanes[h], qc, jnp.zeros_like(qc))
            s = lax.dot_general(qm, kc, nt, preferred_element_type=F32) * dm_ref[h]
            yh = jnp.dot(s.astype(BF16), vc, preferred_element_type=F32)
            yh = yh + jnp.dot(qm, states[h].astype(BF16), preferred_element_type=F32) * gf_ref[h]
            kw = (kc.astype(F32) * wf_ref[h]).astype(BF16)
            new_states.append(dec_ref[h, 0:1, :] * states[h]
                              + lax.dot_general(kw, vc, tn, preferred_element_type=F32))
            y = jnp.where(head_lanes[h], yh, y)
        y_ref[pl.ds(c0, chunk), :] = y
        return tuple(new_states)

    z = jnp.zeros((LANES, LANES), F32)
    unroll = math.gcd(n, 4)
    lax.fori_loop(0, n, fwd, (z, z), unroll=unroll)

    def bwd(ii, states):
        i = n - 1 - ii
        c0 = pl.multiple_of(i * chunk, chunk)
        qc, kc, vc = load(q_ref, c0), load(k_ref, c0), load(v_ref, c0)
        y = y_ref[pl.ds(c0, chunk), :]
        new_states = []
        for h in range(2):
            qm = jnp.where(head_lanes[h], qc, jnp.zeros_like(qc))
            yh = jnp.dot(qm, states[h].astype(BF16), preferred_element_type=F32) * gb_ref[h]
            kw = (kc.astype(F32) * wb_ref[h]).astype(BF16)
            new_states.append(dec_ref[h, 1:2, :] * states[h]
                              + lax.dot_general(kw, vc, tn, preferred_element_type=F32))
            y = y + jnp.where(head_lanes[h], yh, 0.0)
        inv = 1.0 / RET_HEAD_DIM
        mu = [jnp.sum(jnp.where(m, y, 0.0), axis=1, keepdims=True) * inv for m in head_lanes]
        yc = y - jnp.where(head_lanes[0], mu[0], mu[1])
        var = [jnp.sum(jnp.where(m, yc * yc, 0.0), axis=1, keepdims=True) * inv for m in head_lanes]
        yn = yc * lax.rsqrt(jnp.where(head_lanes[0], var[0], var[1]) + LN_EPS)
        g = load(g_ref, c0).astype(F32)
        o_ref[0, pl.ds(c0, chunk), :] = (g * jax.nn.sigmoid(g) * yn).astype(o_ref.dtype)
        return tuple(new_states)

    lax.fori_loop(0, n, bwd, (z, z), unroll=unroll)


def _ret_tables(chunk):
    hidx = jnp.arange(RET_HEADS, dtype=F32)
    log_gf = jnp.log1p(-jnp.exp2(-5.0 - hidx))
    log_gb = jnp.log1p(-jnp.exp2(-5.5 - hidx))
    j = jnp.arange(chunk, dtype=F32)
    rel = j[:, None] - j[None, :]
    dmask = jnp.exp(jnp.where(rel >= 0, log_gf[:, None, None], log_gb[:, None, None]) * jnp.abs(rel))
    rows = lambda t: jnp.broadcast_to(t[:, :, None], (RET_HEADS, chunk, LANES))
    wf = rows(jnp.exp(log_gf[:, None] * (chunk - 1.0 - j)))
    wb = rows(jnp.exp(log_gb[:, None] * j))
    gf = rows(jnp.exp(log_gf[:, None] * (j + 1.0)))
    gb = rows(jnp.exp(log_gb[:, None] * (chunk - j)))
    dec = jnp.stack([jnp.exp(log_gf * chunk), jnp.exp(log_gb * chunk)], axis=1)
    dec = jnp.broadcast_to(jnp.pad(dec, ((0, 0), (0, SUBLANES - 2)))[:, :, None], (RET_HEADS, SUBLANES, LANES))
    return dmask, wf, wb, gf, gb, dec


def _ret_call(qr, kr, vr, gr, tables, chunk):
    b, l, _ = qr.shape
    dmask, wf, wb, gf, gb, dec = tables
    seq = lambda bi, p: (bi, 0, p)
    pair = lambda bi, p: (p, 0, 0)
    seq_spec = pl.BlockSpec((1, l, LANES), seq)
    row_spec = pl.BlockSpec((2, chunk, LANES), pair)
    return pl.pallas_call(
        functools.partial(_ret_kernel, chunk=chunk),
        grid=(b, RET_HEADS // 2),
        in_specs=[seq_spec, seq_spec, seq_spec, seq_spec,
                  pl.BlockSpec((2, chunk, chunk), pair),
                  row_spec, row_spec, row_spec, row_spec,
                  pl.BlockSpec((2, SUBLANES, LANES), pair)],
        out_specs=seq_spec,
        out_shape=jax.ShapeDtypeStruct((b, l, RET_WIDTH), BF16),
        scratch_shapes=[pltpu.VMEM((l, LANES), F32)],
        compiler_params=_cparams("arbitrary", "arbitrary"),
        name="ret",
    )(qr, kr, vr, gr, dmask, wf, wb, gf, gb, dec)


def _post_kernel(ya_ref, yf_ref, yb_ref, ub_ref, yc_ref, x_ref, dsk_ref, gw_ref, gb_ref,
                 woa_ref, wob_ref, woc_ref, g1_ref, lng_ref, lnb_ref, sc2_ref, sh2_ref, rwt_ref, rb_ref,
                 x1_ref, h2_ref, idx_ref, gate_ref, *, alpha):
    ybw = yb_ref[0]
    hi = ybw.astype(BF16)
    lo = (ybw - hi.astype(F32)).astype(BF16)
    rev = _anti_identity(ybw.shape[0])
    ybw = jnp.dot(rev, hi, preferred_element_type=F32) + jnp.dot(rev, lo, preferred_element_type=F32)
    y = yf_ref[0] + ybw + dsk_ref[...] * ub_ref[0]
    y = jax.nn.gelu(y).astype(BF16)
    z = jnp.dot(y, gw_ref[...], preferred_element_type=F32) + gb_ref[...]
    yb = (z[:, :S5_WIDTH] * jax.nn.sigmoid(z[:, S5_WIDTH:])).astype(BF16)
    mix = (jnp.dot(ya_ref[0], woa_ref[...], preferred_element_type=F32)
           + jnp.dot(yb, wob_ref[...], preferred_element_type=F32)
           + jnp.dot(yc_ref[0], woc_ref[...], preferred_element_type=F32))
    x1 = _ln(alpha * x_ref[0] + (1.0 + g1_ref[0]) * mix) * lng_ref[...] + lnb_ref[...]
    x1_ref[0] = x1
    h2 = _ln(x1) * (1.0 + sc2_ref[0]) + sh2_ref[0]
    _store_rows_as_tiles(h2_ref, h2)
    logits = lax.dot_general(rwt_ref[...], h2, (((1,), (1,)), ((), ())),
                             preferred_element_type=F32, precision=HIGHEST) + rb_ref[...]
    eidx = lax.broadcasted_iota(jnp.int32, logits.shape, 0)
    vals, idxs = [], []
    cur = logits
    for _ in range(TOP_K):
        m = jnp.max(cur, axis=0, keepdims=True)
        i = jnp.min(jnp.where(cur == m, eidx, N_EXPERTS), axis=0, keepdims=True)
        vals.append(m)
        idxs.append(i)
        cur = jnp.where(eidx == i, -jnp.inf, cur)
    tv = jnp.concatenate(vals, axis=0)
    e = jnp.exp(tv - vals[0])
    gate_ref[...] = e / jnp.sum(e, axis=0, keepdims=True)
    idx_ref[...] = jnp.concatenate(idxs, axis=0)


def _post_call(ya, yf, yb, ub, yc, x, dsk, gw, gb, woa, wob, woc, g1, lng, lnb, sc2, sh2, rwt, rb, alpha, tm):
    b, l, d = x.shape
    row = lambda bi, i: (bi, i, 0)
    vec = lambda bi, i: (bi, 0, 0)
    full2 = lambda bi, i: (0, 0)
    nl = l // tm
    tok = lambda bi, i: (0, bi * nl + i)

    def fs(a):
        return pl.BlockSpec(a.shape, full2)

    return pl.pallas_call(
        functools.partial(_post_kernel, alpha=alpha),
        grid=(b, nl),
        in_specs=[
            pl.BlockSpec((1, tm, DIFF_WIDTH), row),
            pl.BlockSpec((1, tm, S5_WIDTH), row),
            pl.BlockSpec((1, tm, S5_WIDTH), lambda bi, i: (bi, nl - 1 - i, 0)),
            pl.BlockSpec((1, tm, S5_WIDTH), row),
            pl.BlockSpec((1, tm, RET_WIDTH), row),
            pl.BlockSpec((1, tm, d), row),
            fs(dsk), fs(gw), fs(gb), fs(woa), fs(wob), fs(woc),
            pl.BlockSpec((1, 1, d), vec),
            fs(lng), fs(lnb),
            pl.BlockSpec((1, 1, d), vec),
            pl.BlockSpec((1, 1, d), vec),
            fs(rwt), fs(rb),
        ],
        out_specs=[
            pl.BlockSpec((1, tm, d), row),
            pl.BlockSpec((tm * TOK_TILE, LANES), lambda bi, i: (bi * nl + i, 0)),
            pl.BlockSpec((TOP_K, tm), tok),
            pl.BlockSpec((TOP_K, tm), tok),
        ],
        out_shape=[
            jax.ShapeDtypeStruct((b, l, d), F32),
            jax.ShapeDtypeStruct((b * l * TOK_TILE, LANES), F32),
            jax.ShapeDtypeStruct((TOP_K, b * l), jnp.int32),
            jax.ShapeDtypeStruct((TOP_K, b * l), F32),
        ],
        compiler_params=_cparams("arbitrary", "arbitrary"),
        name="post",
    )(ya, yf, yb, ub, yc, x, dsk, gw, gb, woa, wob, woc, g1, lng, lnb, sc2, sh2, rwt, rb)


TOK_TILE = D_MODEL // LANES


def _tiles_to_rows(ref, lead, first, n):
    return jnp.concatenate(
        [ref[lead, pl.ds(first * TOK_TILE + s, n, stride=TOK_TILE), :] for s in range(TOK_TILE)], axis=1)


def _store_rows_as_tiles(ref, x):
    n = x.shape[0]
    for s in range(TOK_TILE):
        ref[pl.ds(s, n, stride=TOK_TILE), :] = x[:, s * LANES:(s + 1) * LANES]


GATHER_GROUP = 32
GATHER_SLOTS = 3


def _gather_start(src_hbm, row_ref, n, dst, slot, sem, rolled):
    def start(r, k, row):
        src = pl.multiple_of(row_ref[0, 0, r], TOK_TILE)
        pltpu.make_async_copy(src_hbm.at[pl.ds(src, TOK_TILE), :],
                              dst.at[slot, pl.ds(row, TOK_TILE), :], sem.at[slot]).start(priority=k % 2)

    if not rolled:
        for r in range(n):
            start(r, r, r * TOK_TILE)
        return

    def group(g, c):
        for k in range(GATHER_GROUP):
            r = g * GATHER_GROUP + k
            start(r, k, pl.multiple_of(r * TOK_TILE, TOK_TILE))
        return c

    lax.fori_loop(0, n // GATHER_GROUP, group, 0)


def _gather_wait(dst, slot, sem):
    pltpu.make_async_copy(dst.at[(slot + 1) % GATHER_SLOTS], dst.at[slot], sem.at[slot]).wait()


def _gather_pipeline(src_hbm, tables, n, buf, sem, body):
    i = pl.program_id(0)
    slot = i % GATHER_SLOTS
    ahead = (i + 2) % GATHER_SLOTS

    @pl.when(i == 0)
    def _():
        _gather_start(src_hbm, tables[0], n, buf, 0, sem, True)
        _gather_start(src_hbm, tables[1], n, buf, 1, sem, True)

    _gather_wait(buf, slot, sem)
    body(slot, lambda rolled: _gather_start(src_hbm, tables[2], n, buf, ahead, sem, rolled))

    @pl.when(i == pl.num_programs(0) - 1)
    def _():
        _gather_wait(buf, (i + 1) % GATHER_SLOTS, sem)
        _gather_wait(buf, ahead, sem)


def _expert_kernel(be_ref, nu_ref, row0_ref, row1_ref, row2_ref, h_hbm, w1_ref, b1_ref, w2_ref, b2_ref, o_ref,
                   buf, sem):
    eb = row0_ref.shape[2]
    ff = w2_ref.shape[1]
    active = pl.program_id(0) < nu_ref[0]

    def body(slot, start_ahead):
        @pl.when(active)
        def _():
            start_ahead(False)
            hb = _tiles_to_rows(buf, slot, 0, eb).astype(BF16)
            z = jnp.dot(hb, w1_ref[0], preferred_element_type=F32) + b1_ref[0]
            acts = []
            for j in range(ff // LANES):
                z_glu = jnp.minimum(z[:, 2 * j * LANES:(2 * j + 1) * LANES], SWIGLU_LIMIT)
                z_lin = jnp.clip(z[:, (2 * j + 1) * LANES:(2 * j + 2) * LANES], -SWIGLU_LIMIT, SWIGLU_LIMIT)
                acts.append(z_glu * jax.nn.sigmoid(SWIGLU_ALPHA * z_glu) * (z_lin + 1.0))
            act = jnp.concatenate(acts, axis=1).astype(BF16)
            _store_rows_as_tiles(o_ref, jnp.dot(act, w2_ref[0], preferred_element_type=F32) + b2_ref[0])

        @pl.when(jnp.logical_not(active))
        def _():
            start_ahead(True)
            o_ref[...] = jnp.zeros_like(o_ref)

    _gather_pipeline(h_hbm, (row0_ref, row1_ref, row2_ref), eb, buf, sem, body)


def _expert_call(block_e, n_used, slot_row, h2_tiles, w1, b1, w2, b2, layer):
    n_blocks = block_e.shape[0]
    wmap = lambda i, be, nu: (layer * N_EXPERTS + be[i], 0, 0)
    d = w1.shape[1]
    ff = w2.shape[1]
    eb = EXPERT_BLOCK
    rows3 = slot_row.reshape(n_blocks, 1, eb)
    grid_spec = pltpu.PrefetchScalarGridSpec(
        num_scalar_prefetch=2,
        grid=(n_blocks,),
        in_specs=[
            pl.BlockSpec((1, 1, eb), lambda i, be, nu: (i, 0, 0), memory_space=pltpu.SMEM),
            pl.BlockSpec((1, 1, eb), lambda i, be, nu: (jnp.minimum(i + 1, n_blocks - 1), 0, 0),
                         memory_space=pltpu.SMEM),
            pl.BlockSpec((1, 1, eb), lambda i, be, nu: (jnp.minimum(i + 2, n_blocks - 1), 0, 0),
                         memory_space=pltpu.SMEM),
            pl.BlockSpec(memory_space=pl.ANY),
            pl.BlockSpec((1, d, 2 * ff), wmap),
            pl.BlockSpec((1, 1, 2 * ff), wmap),
            pl.BlockSpec((1, ff, d), wmap),
            pl.BlockSpec((1, 1, d), wmap),
        ],
        out_specs=pl.BlockSpec((eb * TOK_TILE, LANES), lambda i, be, nu: (i, 0)),
        scratch_shapes=[pltpu.VMEM((GATHER_SLOTS, eb * TOK_TILE, LANES), F32),
                        pltpu.SemaphoreType.DMA((GATHER_SLOTS,))],
    )
    return pl.pallas_call(
        _expert_kernel,
        grid_spec=grid_spec,
        out_shape=jax.ShapeDtypeStruct((n_blocks * eb * TOK_TILE, LANES), F32),
        compiler_params=_cparams("arbitrary"),
        name="expert",
    )(block_e, n_used, rows3, rows3, rows3, h2_tiles, w1, b1, w2, b2)


def _w1prep_kernel(w_ref, p_ref, o_ref):
    for j in range(w_ref.shape[1] // (2 * LANES)):
        cols = slice(2 * j * LANES, 2 * (j + 1) * LANES)
        o_ref[:, cols] = jnp.dot(w_ref[:, cols].astype(BF16), p_ref[...], preferred_element_type=F32).astype(BF16)


def _w1prep_call(w1_rows, tr):
    n, c = w1_rows.shape
    src = jnp.arange(2 * LANES, dtype=jnp.int32)
    src = jnp.where(src < LANES, 2 * src, 2 * (src - LANES) + 1)
    perm = (jnp.arange(2 * LANES, dtype=jnp.int32)[:, None] == src[None, :]).astype(BF16)
    return pl.pallas_call(
        _w1prep_kernel,
        grid=(n // tr,),
        in_specs=[pl.BlockSpec((tr, c), lambda i: (i, 0)), pl.BlockSpec(perm.shape, lambda i: (0, 0))],
        out_specs=pl.BlockSpec((tr, c), lambda i: (i, 0)),
        out_shape=jax.ShapeDtypeStruct((n, c), BF16),
        compiler_params=_cparams("arbitrary"),
        name="w1prep",
    )(w1_rows, perm)


def _combine_kernel(row0_ref, row1_ref, row2_ref, ys_hbm, gate_ref, x1_ref, g2_ref, lng_ref, lnb_ref, o_ref,
                    buf, sem, *, alpha):
    tm = x1_ref.shape[0]

    def body(slot, start_ahead):
        start_ahead(False)
        gates = gate_ref[...]
        ffo = _tiles_to_rows(buf, slot, 0, tm) * gates[:, 0:1]
        for k in range(1, TOP_K):
            ffo = ffo + _tiles_to_rows(buf, slot, k * tm, tm) * gates[:, k:k + 1]
        o_ref[...] = _ln(alpha * x1_ref[...] + (1.0 + g2_ref[0]) * ffo) * lng_ref[...] + lnb_ref[...]

    _gather_pipeline(ys_hbm, (row0_ref, row1_ref, row2_ref), tm * TOP_K, buf, sem, body)


def _combine_call(dest_row, ys_tiles, gates_t, x1, g2, lng, lnb, alpha, tm):
    b, l, d = x1.shape
    nl = l // tm
    n_tiles = b * nl
    rows3 = dest_row.reshape(n_tiles, tm, TOP_K).transpose(0, 2, 1).reshape(n_tiles, 1, tm * TOP_K)
    full2 = lambda i: (0, 0)
    return pl.pallas_call(
        functools.partial(_combine_kernel, alpha=alpha),
        grid=(n_tiles,),
        in_specs=[
            pl.BlockSpec((1, 1, tm * TOP_K), lambda i: (i, 0, 0), memory_space=pltpu.SMEM),
            pl.BlockSpec((1, 1, tm * TOP_K), lambda i: (jnp.minimum(i + 1, n_tiles - 1), 0, 0),
                         memory_space=pltpu.SMEM),
            pl.BlockSpec((1, 1, tm * TOP_K), lambda i: (jnp.minimum(i + 2, n_tiles - 1), 0, 0),
                         memory_space=pltpu.SMEM),
            pl.BlockSpec(memory_space=pl.ANY),
            pl.BlockSpec((tm, TOP_K), lambda i: (i, 0)),
            pl.BlockSpec((tm, d), lambda i: (i, 0)),
            pl.BlockSpec((1, 1, d), lambda i: (i // nl, 0, 0)),
            pl.BlockSpec(lng.shape, full2),
            pl.BlockSpec(lnb.shape, full2),
        ],
        out_specs=pl.BlockSpec((tm, d), lambda i: (i, 0)),
        out_shape=jax.ShapeDtypeStruct((b * l, d), F32),
        scratch_shapes=[pltpu.VMEM((GATHER_SLOTS, TOP_K * tm * TOK_TILE, LANES), F32),
                        pltpu.SemaphoreType.DMA((GATHER_SLOTS,))],
        compiler_params=_cparams("arbitrary"),
        name="combine",
    )(rows3, rows3, rows3, ys_tiles, gates_t, x1.reshape(b * l, d), g2, lng, lnb).reshape(b, l, d)


def _route(top_idx, n_tok):
    n_assign = n_tok * TOP_K
    n_blocks = n_assign // EXPERT_BLOCK + N_EXPERTS
    e_flat = top_idx.T.reshape(-1)
    order = jnp.argsort(e_flat)
    e_sorted = e_flat[order]
    counts = jnp.sum((e_flat[:, None] == jnp.arange(N_EXPERTS, dtype=jnp.int32)[None, :]).astype(jnp.int32), axis=0)
    start = jnp.cumsum(counts) - counts
    padded = (counts + EXPERT_BLOCK - 1) // EXPERT_BLOCK * EXPERT_BLOCK
    pend = jnp.cumsum(padded)
    pstart = pend - padded
    dest_sorted = pstart[e_sorted] + jnp.arange(n_assign, dtype=jnp.int32) - start[e_sorted]
    _, dest = lax.sort((order.astype(jnp.int32), dest_sorted.astype(jnp.int32)), num_keys=1)
    block_start = jnp.arange(n_blocks, dtype=jnp.int32) * EXPERT_BLOCK
    block_e = jnp.sum((pend[None, :] <= block_start[:, None]).astype(jnp.int32), axis=1)
    block_e = jnp.minimum(block_e, N_EXPERTS - 1)
    slot = jnp.arange(n_blocks * EXPERT_BLOCK, dtype=jnp.int32)
    slot_e = jnp.repeat(block_e, EXPERT_BLOCK)
    within = slot - pstart[slot_e]
    valid = (within < counts[slot_e]) & (slot < pend[-1])
    src = jnp.clip(start[slot_e] + within, 0, n_assign - 1)
    slot_tok = jnp.where(valid, (order[src] // TOP_K).astype(jnp.int32), 0)
    n_used = (pend[-1] // EXPERT_BLOCK).astype(jnp.int32).reshape(1)
    return block_e, n_used, slot_tok, dest


def _pick(n, pref):
    return pref if n % pref == 0 else n


def kernel(x, c, positions, ada_w, ada_b, w_in, w_out, diff_lambda_q1, diff_lambda_k1, diff_lambda_q2,
           diff_lambda_k2, diff_subln_g, s5_lambda_re, s5_lambda_im, s5_log_step, s5_b_re, s5_b_im, s5_c_re,
           s5_c_im, s5_d, s5_glu_w, s5_glu_b, ln_g, ln_b, router_w, router_b, exp_w1, exp_b1, exp_w2, exp_b2):
    bsz, seq, d = x.shape
    depth = ada_w.shape[0]
    n_tok = bsz * seq
    alpha = (2 * depth) ** 0.25

    def tables(inv_freq):
        ang = positions.astype(F32)[..., None] * inv_freq
        cs, sn = jnp.cos(ang), jnp.sin(ang)
        return jnp.concatenate([cs, cs, cs, cs], axis=-1), jnp.concatenate([-sn, sn, -sn, sn], axis=-1)

    cos_a, sin_a = tables(1.0 / (ROPE_THETA ** (jnp.arange(0, DIFF_HEAD_DIM, 2, dtype=F32) / DIFF_HEAD_DIM)))
    cos_r, sin_r = tables(1.0 / (ROPE_THETA ** jnp.linspace(0.0, 1.0, RET_HEAD_DIM // 2, dtype=F32)))
    ret_tables = _ret_tables(RET_CHUNK)

    c_pad = jnp.pad(c.astype(F32), ((0, SUBLANES - bsz), (0, 0)))
    mod = _ada_call(c_pad, ada_w, ada_b)[:, :bsz]

    tm_pre = 512 if seq % 1024 == 0 else seq // 2
    tq = _pick(seq, 256)
    tc = _pick(seq, 128)
    tm_post = _pick(seq, 512) if seq > 512 else seq // 2
    tm_comb = _pick(seq, 128)

    n_e, two_ff = exp_w1.shape[1], exp_w1.shape[3]
    w1 = _w1prep_call(exp_w1.reshape(depth * n_e * d, two_ff), 512).reshape(depth * n_e, d, two_ff)
    b1 = exp_b1.reshape(depth * n_e, two_ff // (2 * LANES), LANES, 2).transpose(0, 1, 3, 2)
    b1 = b1.reshape(depth * n_e, 1, two_ff).astype(F32)
    w2 = exp_w2.astype(BF16).reshape(depth * n_e, exp_w2.shape[2], d)
    b2 = exp_b2.reshape(depth * n_e, 1, d).astype(F32)

    for l in range(depth):
        sh1, sc1, g1, sh2, sc2, g2 = [mod[l, :, None, i * d:(i + 1) * d] for i in range(6)]
        qa, ka, va, ub, ubr, qr, kr, vr, gr = _pre_call(
            x, sc1, sh1, w_in[l].astype(BF16), cos_a, sin_a, cos_r, sin_r, tm_pre)

        lam_init = 0.8 - 0.6 * math.exp(-0.3 * l)
        ya = _attn_call(qa, ka, va, diff_subln_g[l][None, :].astype(F32),
                        diff_lambda_q1[l][None, :], diff_lambda_k1[l][None, :],
                        diff_lambda_q2[l][None, :], diff_lambda_k2[l][None, :], lam_init, tq)

        bmat_f, bmat_b, cmat, a_rows = _s5_params(
            s5_lambda_re[l], s5_lambda_im[l], s5_log_step[l], s5_b_re[l], s5_b_im[l], s5_c_re[l], s5_c_im[l], bsz)
        y_fwd, y_bwd = _s5_call(ub, ubr, bmat_f, bmat_b, cmat, a_rows, tc)

        yc = _ret_call(qr, kr, vr, gr, ret_tables, RET_CHUNK)

        wo = w_out[l].astype(BF16)
        x1, h2, top_idx, gates = _post_call(
            ya, y_fwd, y_bwd, ub, yc, x,
            s5_d[l][None, :].astype(F32), s5_glu_w[l].astype(BF16), s5_glu_b[l][None, :].astype(F32),
            wo[:DIFF_WIDTH], wo[DIFF_WIDTH:DIFF_WIDTH + S5_WIDTH], wo[DIFF_WIDTH + S5_WIDTH:],
            g1, ln_g[l, 0][None, :], ln_b[l, 0][None, :], sc2, sh2,
            router_w[l].T.astype(F32), router_b[l][:, None].astype(F32), alpha, tm_post)

        block_e, n_used, slot_tok, dest = _route(top_idx, n_tok)
        ys = _expert_call(block_e, n_used, slot_tok * TOK_TILE, h2, w1, b1, w2, b2, l)
        x = _combine_call(dest * TOK_TILE, ys, gates.T, x1, g2, ln_g[l, 1][None, :], ln_b[l, 1][None, :],
                          alpha, tm_comb)
    return x
```

```python
import functools
import math

import jax
import jax.numpy as jnp
from jax import lax
from jax.experimental import pallas as pl
from jax.experimental.pallas import tpu as pltpu

F32 = jnp.float32
BF16 = jnp.bfloat16
HIGHEST = lax.Precision.HIGHEST

LANES = 128
SUBLANES = 8
VMEM_LIMIT = 56 * 1024 * 1024

D_MODEL = 1024
DIFF_HEADS = 4
DIFF_WIDTH = D_MODEL // 2
DIFF_V_DIM = DIFF_WIDTH // DIFF_HEADS
DIFF_HEAD_DIM = DIFF_V_DIM // 2
DIFF_QK = DIFF_HEADS * 2 * DIFF_HEAD_DIM
ROPE_THETA = 10000.0
S5_WIDTH = D_MODEL // 4
S5_GROUP = 16
S5_GROUPS = S5_WIDTH // S5_GROUP
S5_STATE = 64
S5_NS = S5_GROUPS * S5_STATE
RET_HEADS = 4
RET_WIDTH = D_MODEL // 4
RET_HEAD_DIM = RET_WIDTH // RET_HEADS
RET_CHUNK = 128
IN_WIDTH = 2 * DIFF_QK + DIFF_WIDTH + S5_WIDTH + 4 * RET_WIDTH
N_EXPERTS = 32
TOP_K = 4
EXPERT_FF = D_MODEL
SWIGLU_ALPHA = 1.702
SWIGLU_LIMIT = 7.0
EXPERT_BLOCK = 256
LN_EPS = 1e-5
NEG_BIG = -1e30

OFF_QA = 0
OFF_KA = OFF_QA + DIFF_QK
OFF_VA = OFF_KA + DIFF_QK
OFF_UB = OFF_VA + DIFF_WIDTH
OFF_QR = OFF_UB + S5_WIDTH
OFF_KR = OFF_QR + RET_WIDTH
OFF_VR = OFF_KR + RET_WIDTH
OFF_GR = OFF_VR + RET_WIDTH


def _cparams(*sem):
    return pltpu.CompilerParams(dimension_semantics=sem, vmem_limit_bytes=VMEM_LIMIT)


def _ln(x):
    mu = jnp.mean(x, axis=-1, keepdims=True)
    xc = x - mu
    var = jnp.mean(xc * xc, axis=-1, keepdims=True)
    return xc * lax.rsqrt(var + LN_EPS)


def _ada_kernel(c_ref, w_ref, b_ref, o_ref):
    c = c_ref[...]
    cond = c * jax.nn.sigmoid(c)
    o_ref[0] = jnp.dot(cond, w_ref[0], preferred_element_type=F32, precision=HIGHEST) + b_ref[0]


def _ada_call(c_pad, ada_w, ada_b):
    depth, d, n = ada_w.shape
    tn = n // 6
    return pl.pallas_call(
        _ada_kernel,
        grid=(depth, n // tn),
        in_specs=[
            pl.BlockSpec((SUBLANES, d), lambda l, j: (0, 0)),
            pl.BlockSpec((1, d, tn), lambda l, j: (l, 0, j)),
            pl.BlockSpec((1, 1, tn), lambda l, j: (l, 0, j)),
        ],
        out_specs=pl.BlockSpec((1, SUBLANES, tn), lambda l, j: (l, 0, j)),
        out_shape=jax.ShapeDtypeStruct((depth, SUBLANES, n), F32),
        compiler_params=_cparams("arbitrary", "arbitrary"),
        name="ada",
    )(c_pad, ada_w, ada_b.reshape(depth, 1, n))


def _rope_lanes(z, cos, sin_signed, first_half):
    partner = jnp.where(first_half, pltpu.roll(z, LANES - 32, 1), pltpu.roll(z, 32, 1))
    return z * cos + partner * sin_signed


def _anti_identity(n):
    r = lax.broadcasted_iota(jnp.int32, (n, n), 0)
    c = lax.broadcasted_iota(jnp.int32, (n, n), 1)
    return (r + c == n - 1).astype(BF16)


def _pre_kernel(x_ref, sc_ref, sh_ref, w_ref, ca_ref, sa_ref, cr_ref, sr_ref,
                qa_ref, ka_ref, va_ref, ub_ref, ubr_ref, qr_ref, kr_ref, vr_ref, gr_ref):
    tm = x_ref.shape[1]
    h = (_ln(x_ref[0]) * (1.0 + sc_ref[0]) + sh_ref[0]).astype(BF16)
    lane = lax.broadcasted_iota(jnp.int32, (tm, LANES), 1)
    first_half = (lane % 64) < 32

    def proj(off, width):
        return jnp.dot(h, w_ref[:, off:off + width], preferred_element_type=F32)

    def rope_store(z, cos, sin, scale, out_ref):
        for j in range(z.shape[1] // LANES):
            zj = _rope_lanes(z[:, j * LANES:(j + 1) * LANES], cos, sin, first_half)
            out_ref[0, :, j * LANES:(j + 1) * LANES] = (zj * scale).astype(out_ref.dtype)

    ca, sa, cr, sr = ca_ref[0], sa_ref[0], cr_ref[0], sr_ref[0]
    zq = proj(OFF_QA, DIFF_QK)
    for j in range(DIFF_QK // LANES):
        zj = _rope_lanes(zq[:, j * LANES:(j + 1) * LANES], ca, sa, first_half)
        qa_ref[0, j * LANES:(j + 1) * LANES, :] = (zj * (DIFF_HEAD_DIM ** -0.5 * math.log2(math.e))).T.astype(BF16)
    rope_store(proj(OFF_KA, DIFF_QK), ca, sa, 1.0, ka_ref)
    zv = proj(OFF_VA, DIFF_WIDTH)
    for j in range(DIFF_WIDTH // LANES):
        va_ref[0, 0, j * LANES:(j + 1) * LANES, :] = zv[:, j * LANES:(j + 1) * LANES].T.astype(BF16)
    u = proj(OFF_UB, S5_WIDTH)
    ub_ref[0] = u
    ubr_ref[0] = jnp.dot(_anti_identity(tm), u.astype(BF16), preferred_element_type=F32).astype(BF16)
    rope_store(proj(OFF_QR, RET_WIDTH), cr, sr, 1.0, qr_ref)
    rope_store(proj(OFF_KR, RET_WIDTH), cr, sr, RET_HEAD_DIM ** -0.5, kr_ref)
    vr_ref[0] = proj(OFF_VR, RET_WIDTH).astype(vr_ref.dtype)
    gr_ref[0] = proj(OFF_GR, RET_WIDTH).astype(gr_ref.dtype)


def _pre_call(x, sc, sh, w_in_bf, cos_a, sin_a, cos_r, sin_r, tm):
    b, l, d = x.shape
    row = lambda bi, i: (bi, i, 0)
    vec = lambda bi, i: (bi, 0, 0)
    nl = l // tm
    mirrored = lambda bi, i: (bi, nl - 1 - i, 0)
    widths = (DIFF_QK, DIFF_QK, DIFF_WIDTH, S5_WIDTH, S5_WIDTH, RET_WIDTH, RET_WIDTH, RET_WIDTH, RET_WIDTH)
    dtypes = (BF16, BF16, BF16, F32, BF16, BF16, BF16, BF16, BF16)
    maps = (row, row, row, row, mirrored, row, row, row, row)
    specs = [pl.BlockSpec((1, tm, w), m) for w, m in zip(widths, maps)]
    shapes = [jax.ShapeDtypeStruct((b, l, w), dt) for w, dt in zip(widths, dtypes)]
    specs[0] = pl.BlockSpec((1, DIFF_QK, tm), lambda bi, i: (bi, 0, i))
    shapes[0] = jax.ShapeDtypeStruct((b, DIFF_QK, l), BF16)
    specs[2] = pl.BlockSpec((1, 1, DIFF_WIDTH, tm), lambda bi, i: (bi, i, 0, 0))
    shapes[2] = jax.ShapeDtypeStruct((b, nl, DIFF_WIDTH, tm), BF16)
    return pl.pallas_call(
        _pre_kernel,
        grid=(b, l // tm),
        in_specs=[
            pl.BlockSpec((1, tm, d), row),
            pl.BlockSpec((1, 1, d), vec),
            pl.BlockSpec((1, 1, d), vec),
            pl.BlockSpec((d, IN_WIDTH), lambda bi, i: (0, 0)),
            pl.BlockSpec((1, tm, LANES), row),
            pl.BlockSpec((1, tm, LANES), row),
            pl.BlockSpec((1, tm, LANES), row),
            pl.BlockSpec((1, tm, LANES), row),
        ],
        out_specs=specs,
        out_shape=shapes,
        compiler_params=_cparams("arbitrary", "arbitrary"),
        name="pre",
    )(x, sc, sh, w_in_bf, cos_a, sin_a, cos_r, sin_r)


ATTN_CHUNKS_PER_TRIP = 16
ONES_ROWS = 16


def _attn_kernel(qt_ref, k_ref, vt_ref, g_ref, lq1_ref, lk1_ref, lq2_ref, lk2_ref, o_ref, s_a, s_b, *, lam_init):
    tq = qt_ref.shape[2]
    nk, _, tk = vt_ref.shape[1:]
    qt = qt_ref[0]
    row = lax.broadcasted_iota(jnp.int32, qt.shape, 0)
    zero = jnp.zeros_like(qt)
    qq = jnp.concatenate([jnp.where(row < DIFF_HEAD_DIM, qt, zero), jnp.where(row >= DIFF_HEAD_DIM, qt, zero)],
                         axis=1)
    ones = jnp.ones((ONES_ROWS, tk), BF16)

    def scores(j, s_ref):
        start = pl.multiple_of(j * tk, tk)
        s = jnp.dot(k_ref[0, pl.ds(start, tk), :], qq, preferred_element_type=F32)
        s_ref[...] = s
        return jnp.max(s, axis=0, keepdims=True)

    def update(j, s_ref, mc, m, acc):
        m_new = jnp.maximum(m, mc)
        p = jnp.exp2(s_ref[...] - m_new).astype(BF16)
        vt = jnp.concatenate([vt_ref[0, j], ones], axis=0)
        return m_new, jnp.exp2(m - m_new) * acc + jnp.dot(vt, p, preferred_element_type=F32)

    per_trip = math.gcd(nk, ATTN_CHUNKS_PER_TRIP)
    bufs = (s_a, s_b)

    def body(jj, carry):
        mc, m, acc = carry
        j = per_trip * jj
        for c in range(per_trip):
            mc_next = scores(jnp.minimum(j + c + 1, nk - 1), bufs[(c + 1) % 2])
            m, acc = update(j + c, bufs[c % 2], mc, m, acc)
            mc = mc_next
        return mc, m, acc

    m0 = jnp.full((1, 2 * tq), NEG_BIG, F32)
    a0 = jnp.zeros((DIFF_V_DIM + ONES_ROWS, 2 * tq), F32)
    _, _, acc = lax.fori_loop(0, nk // per_trip, body, (scores(0, s_a), m0, a0))
    lam = (jnp.exp(jnp.sum(lq1_ref[...] * lk1_ref[...], axis=1, keepdims=True))
           - jnp.exp(jnp.sum(lq2_ref[...] * lk2_ref[...], axis=1, keepdims=True)) + lam_init)
    on = acc[:DIFF_V_DIM] / acc[DIFF_V_DIM:DIFF_V_DIM + 1]
    ot = on[:, :tq] - lam * on[:, tq:]
    ot = ot * lax.rsqrt(jnp.mean(ot * ot, axis=0, keepdims=True) + LN_EPS)
    o_ref[0] = (ot.T * g_ref[...] * (1.0 - lam_init)).astype(o_ref.dtype)


def _attn_call(qt, ka, vt, subln_g, lq1, lk1, lq2, lk2, lam_init, tq):
    b, l, _ = ka.shape
    nk, tk = vt.shape[1], vt.shape[3]
    cmap = lambda bi, h, i: (0, 0)
    small = pl.BlockSpec((1, DIFF_HEAD_DIM), cmap)
    return pl.pallas_call(
        functools.partial(_attn_kernel, lam_init=lam_init),
        grid=(b, DIFF_HEADS, l // tq),
        in_specs=[
            pl.BlockSpec((1, LANES, tq), lambda bi, h, i: (bi, h, i)),
            pl.BlockSpec((1, l, LANES), lambda bi, h, i: (bi, 0, h)),
            pl.BlockSpec((1, nk, LANES, tk), lambda bi, h, i: (bi, 0, h, 0)),
            pl.BlockSpec((1, DIFF_V_DIM), cmap),
            small, small, small, small,
        ],
        out_specs=pl.BlockSpec((1, tq, LANES), lambda bi, h, i: (bi, i, h)),
        out_shape=jax.ShapeDtypeStruct((b, l, DIFF_WIDTH), BF16),
        scratch_shapes=[pltpu.VMEM((tk, 2 * tq), F32), pltpu.VMEM((tk, 2 * tq), F32)],
        compiler_params=_cparams("arbitrary", "arbitrary", "arbitrary"),
        name="attn",
    )(qt, ka, vt, subln_g, lq1, lk1, lq2, lk2)


def _s5_kernel(uf_ref, ub_ref, bf_ref, bb_ref, c_ref, a_ref, yf_ref, yb_ref, xs_ref, st_ref, *, tc):
    n = S5_NS
    nc = n // LANES
    batch = uf_ref.shape[0]

    @pl.when(pl.program_id(0) == 0)
    def _():
        st_ref[...] = jnp.zeros_like(st_ref)

    for j in range(SUBLANES):
        src = uf_ref[j].astype(BF16) if j < batch else ub_ref[j - batch]
        x = jnp.dot(src, bf_ref[...] if j < batch else bb_ref[...], preferred_element_type=F32)
        for c in range(2 * nc):
            xs_ref[c, pl.ds(j, tc, stride=SUBLANES), :] = x[:, c * LANES:(c + 1) * LANES]

    a_re = a_ref[:, :n]
    a_im = a_ref[:, n:]

    def step(t, carry):
        h_re, h_im = carry
        r0 = pl.multiple_of(t * SUBLANES, SUBLANES)
        x_re = jnp.concatenate([xs_ref[c, pl.ds(r0, SUBLANES), :] for c in range(nc)], axis=1)
        x_im = jnp.concatenate([xs_ref[nc + c, pl.ds(r0, SUBLANES), :] for c in range(nc)], axis=1)
        n_re = a_re * h_re - a_im * h_im + x_re
        n_im = a_re * h_im + a_im * h_re + x_im
        for c in range(nc):
            xs_ref[c, pl.ds(r0, SUBLANES), :] = n_re[:, c * LANES:(c + 1) * LANES]
            xs_ref[nc + c, pl.ds(r0, SUBLANES), :] = n_im[:, c * LANES:(c + 1) * LANES]
        return n_re, n_im

    h_re, h_im = lax.fori_loop(0, tc, step, (st_ref[:, :n], st_ref[:, n:]), unroll=4)
    st_ref[:, :n] = h_re
    st_ref[:, n:] = h_im

    for j in range(SUBLANES):
        d, bi = divmod(j, batch)
        hs = jnp.concatenate([xs_ref[c, pl.ds(j, tc, stride=SUBLANES), :] for c in range(2 * nc)], axis=1)
        yj = jnp.dot(hs.astype(BF16), c_ref[:, d * S5_WIDTH:(d + 1) * S5_WIDTH], preferred_element_type=F32)
        if d == 0:
            yf_ref[bi] = yj
        else:
            yb_ref[bi] = yj


def _s5_call(ub, ubr, bmat_f, bmat_b, cmat, a_rows, tc):
    bsz, l, w = ub.shape
    assert 2 * bsz == SUBLANES
    rows = tc * SUBLANES
    full = lambda i: (0, 0)
    seq_spec = pl.BlockSpec((bsz, tc, w), lambda i: (0, i, 0))
    return pl.pallas_call(
        functools.partial(_s5_kernel, tc=tc),
        grid=(l // tc,),
        in_specs=[
            seq_spec, seq_spec,
            pl.BlockSpec(bmat_f.shape, full),
            pl.BlockSpec(bmat_b.shape, full),
            pl.BlockSpec(cmat.shape, full),
            pl.BlockSpec(a_rows.shape, full),
        ],
        out_specs=[seq_spec, seq_spec],
        out_shape=[jax.ShapeDtypeStruct((bsz, l, w), F32), jax.ShapeDtypeStruct((bsz, l, w), F32)],
        scratch_shapes=[pltpu.VMEM((2 * S5_NS // LANES, rows, LANES), F32), pltpu.VMEM((SUBLANES, 2 * S5_NS), F32)],
        compiler_params=_cparams("arbitrary"),
        name="s5",
    )(ub, ubr, bmat_f, bmat_b, cmat, a_rows)


def _s5_params(lam_re, lam_im, log_step, b_re, b_im, c_re, c_im, batch):
    lr = jnp.minimum(lam_re.astype(F32), -1e-4)
    li = lam_im.astype(F32)
    dt = jnp.exp(log_step.astype(F32))[..., None]
    mag = jnp.exp(lr * dt)
    ab_re = mag * jnp.cos(li * dt)
    ab_im = mag * jnp.sin(li * dt)
    den = lr * lr + li * li
    f_re = ((ab_re - 1.0) * lr + ab_im * li) / den
    f_im = (ab_im * lr - (ab_re - 1.0) * li) / den
    bre = b_re.astype(F32)
    bim = b_im.astype(F32)
    bb_re = f_re[..., None] * bre - f_im[..., None] * bim
    bb_im = f_re[..., None] * bim + f_im[..., None] * bre
    eye = jnp.eye(S5_GROUPS, dtype=F32)

    def in_mat(m):
        return jnp.einsum('gpn,gh->gnhp', m, eye).reshape(S5_WIDTH, S5_NS)

    def out_mat(m):
        return jnp.einsum('gnp,gh->gphn', m, eye).reshape(S5_NS, S5_WIDTH)

    bmats = [jnp.concatenate([in_mat(bb_re[d]), in_mat(bb_im[d])], axis=1).astype(BF16) for d in range(2)]
    cmat = jnp.concatenate(
        [jnp.concatenate([out_mat(c_re[d].astype(F32)), -out_mat(c_im[d].astype(F32))], axis=0) for d in range(2)],
        axis=1).astype(BF16)
    a_dir = jnp.concatenate([ab_re.reshape(2, S5_NS), ab_im.reshape(2, S5_NS)], axis=1)
    a_rows = jnp.repeat(a_dir, batch, axis=0)
    return bmats[0], bmats[1], cmat, a_rows


def _ret_kernel(q_ref, k_ref, v_ref, g_ref, dm_ref, wf_ref, wb_ref, gf_ref, gb_ref, dec_ref,
                o_ref, y_ref, *, chunk):
    seq = q_ref.shape[1]
    n = seq // chunk
    lane = lax.broadcasted_iota(jnp.int32, (chunk, LANES), 1)
    head_lanes = [lane < RET_HEAD_DIM, lane >= RET_HEAD_DIM]
    nt = (((1,), (1,)), ((), ()))
    tn = (((0,), (0,)), ((), ()))

    def load(ref, c0):
        return ref[0, pl.ds(c0, chunk), :]

    def fwd(i, states):
        c0 = pl.multiple_of(i * chunk, chunk)
        qc, kc, vc = load(q_ref, c0), load(k_ref, c0), load(v_ref, c0)
        y = jnp.zeros((chunk, LANES), F32)
        new_states = []
        for h in range(2):
            qm = jnp.where(head_lanes[h], qc, jnp.zeros_like(qc))
            s = lax.dot_general(qm, kc, nt, preferred_element_type=F32) * dm_ref[h]
            yh = jnp.dot(s.astype(BF16), vc, preferred_element_type=F32)
            yh = yh + jnp.dot(qm, states[h].astype(BF16), preferred_element_type=F32) * gf_ref[h]
            kw = (kc.astype(F32) * wf_ref[h]).astype(BF16)
            new_states.append(dec_ref[h, 0:1, :] * states[h]
                              + lax.dot_general(kw, vc, tn, preferred_element_type=F32))
            y = jnp.where(head_lanes[h], yh, y)
        y_ref[pl.ds(c0, chunk), :] = y
        return tuple(new_states)

    z = jnp.zeros((LANES, LANES), F32)
    unroll = math.gcd(n, 4)
    lax.fori_loop(0, n, fwd, (z, z), unroll=unroll)

    def bwd(ii, states):
        i = n - 1 - ii
        c0 = pl.multiple_of(i * chunk, chunk)
        qc, kc, vc = load(q_ref, c0), load(k_ref, c0), load(v_ref, c0)
        y = y_ref[pl.ds(c0, chunk), :]
        new_states = []
        for h in range(2):
            qm = jnp.where(head_lanes[h], qc, jnp.zeros_like(qc))
            yh = jnp.dot(qm, states[h].astype(BF16), preferred_element_type=F32) * gb_ref[h]
            kw = (kc.astype(F32) * wb_ref[h]).astype(BF16)
            new_states.append(dec_ref[h, 1:2, :] * states[h]
                              + lax.dot_general(kw, vc, tn, preferred_element_type=F32))
            y = y + jnp.where(head_lanes[h], yh, 0.0)
        inv = 1.0 / RET_HEAD_DIM
        mu = [jnp.sum(jnp.where(m, y, 0.0), axis=1, keepdims=True) * inv for m in head_lanes]
        yc = y - jnp.where(head_lanes[0], mu[0], mu[1])
        var = [jnp.sum(jnp.where(m, yc * yc, 0.0), axis=1, keepdims=True) * inv for m in head_lanes]
        yn = yc * lax.rsqrt(jnp.where(head_lanes[0], var[0], var[1]) + LN_EPS)
        g = load(g_ref, c0).astype(F32)
        o_ref[0, pl.ds(c0, chunk), :] = (g * jax.nn.sigmoid(g) * yn).astype(o_ref.dtype)
        return tuple(new_states)

    lax.fori_loop(0, n, bwd, (z, z), unroll=unroll)


def _ret_tables(chunk):
    hidx = jnp.arange(RET_HEADS, dtype=F32)
    log_gf = jnp.log1p(-jnp.exp2(-5.0 - hidx))
    log_gb = jnp.log1p(-jnp.exp2(-5.5 - hidx))
    j = jnp.arange(chunk, dtype=F32)
    rel = j[:, None] - j[None, :]
    dmask = jnp.exp(jnp.where(rel >= 0, log_gf[:, None, None], log_gb[:, None, None]) * jnp.abs(rel))
    rows = lambda t: jnp.broadcast_to(t[:, :, None], (RET_HEADS, chunk, LANES))
    wf = rows(jnp.exp(log_gf[:, None] * (chunk - 1.0 - j)))
    wb = rows(jnp.exp(log_gb[:, None] * j))
    gf = rows(jnp.exp(log_gf[:, None] * (j + 1.0)))
    gb = rows(jnp.exp(log_gb[:, None] * (chunk - j)))
    dec = jnp.stack([jnp.exp(log_gf * chunk), jnp.exp(log_gb * chunk)], axis=1)
    dec = jnp.broadcast_to(jnp.pad(dec, ((0, 0), (0, SUBLANES - 2)))[:, :, None], (RET_HEADS, SUBLANES, LANES))
    return dmask, wf, wb, gf, gb, dec


def _ret_call(qr, kr, vr, gr, tables, chunk):
    b, l, _ = qr.shape
    dmask, wf, wb, gf, gb, dec = tables
    seq = lambda bi, p: (bi, 0, p)
    pair = lambda bi, p: (p, 0, 0)
    seq_spec = pl.BlockSpec((1, l, LANES), seq)
    row_spec = pl.BlockSpec((2, chunk, LANES), pair)
    return pl.pallas_call(
        functools.partial(_ret_kernel, chunk=chunk),
        grid=(b, RET_HEADS // 2),
        in_specs=[seq_spec, seq_spec, seq_spec, seq_spec,
                  pl.BlockSpec((2, chunk, chunk), pair),
                  row_spec, row_spec, row_spec, row_spec,
                  pl.BlockSpec((2, SUBLANES, LANES), pair)],
        out_specs=seq_spec,
        out_shape=jax.ShapeDtypeStruct((b, l, RET_WIDTH), BF16),
        scratch_shapes=[pltpu.VMEM((l, LANES), F32)],
        compiler_params=_cparams("arbitrary", "arbitrary"),
        name="ret",
    )(qr, kr, vr, gr, dmask, wf, wb, gf, gb, dec)


def _post_kernel(ya_ref, yf_ref, yb_ref, ub_ref, yc_ref, x_ref, dsk_ref, gw_ref, gb_ref,
                 woa_ref, wob_ref, woc_ref, g1_ref, lng_ref, lnb_ref, sc2_ref, sh2_ref, rwt_ref, rb_ref,
                 x1_ref, h2_ref, idx_ref, gate_ref, *, alpha):
    ybw = yb_ref[0]
    hi = ybw.astype(BF16)
    lo = (ybw - hi.astype(F32)).astype(BF16)
    rev = _anti_identity(ybw.shape[0])
    ybw = jnp.dot(rev, hi, preferred_element_type=F32) + jnp.dot(rev, lo, preferred_element_type=F32)
    y = yf_ref[0] + ybw + dsk_ref[...] * ub_ref[0]
    y = jax.nn.gelu(y).astype(BF16)
    z = jnp.dot(y, gw_ref[...], preferred_element_type=F32) + gb_ref[...]
    yb = (z[:, :S5_WIDTH] * jax.nn.sigmoid(z[:, S5_WIDTH:])).astype(BF16)
    mix = (jnp.dot(ya_ref[0], woa_ref[...], preferred_element_type=F32)
           + jnp.dot(yb, wob_ref[...], preferred_element_type=F32)
           + jnp.dot(yc_ref[0], woc_ref[...], preferred_element_type=F32))
    x1 = _ln(alpha * x_ref[0] + (1.0 + g1_ref[0]) * mix) * lng_ref[...] + lnb_ref[...]
    x1_ref[0] = x1
    h2 = _ln(x1) * (1.0 + sc2_ref[0]) + sh2_ref[0]
    _store_rows_as_tiles(h2_ref, h2)
    logits = lax.dot_general(rwt_ref[...], h2, (((1,), (1,)), ((), ())),
                             preferred_element_type=F32, precision=HIGHEST) + rb_ref[...]
    eidx = lax.broadcasted_iota(jnp.int32, logits.shape, 0)
    vals, idxs = [], []
    cur = logits
    for _ in range(TOP_K):
        m = jnp.max(cur, axis=0, keepdims=True)
        i = jnp.min(jnp.where(cur == m, eidx, N_EXPERTS), axis=0, keepdims=True)
        vals.append(m)
        idxs.append(i)
        cur = jnp.where(eidx == i, -jnp.inf, cur)
    tv = jnp.concatenate(vals, axis=0)
    e = jnp.exp(tv - vals[0])
    gate_ref[...] = e / jnp.sum(e, axis=0, keepdims=True)
    idx_ref[...] = jnp.concatenate(idxs, axis=0)


def _post_call(ya, yf, yb, ub, yc, x, dsk, gw, gb, woa, wob, woc, g1, lng, lnb, sc2, sh2, rwt, rb, alpha, tm):
    b, l, d = x.shape
    row = lambda bi, i: (bi, i, 0)
    vec = lambda bi, i: (bi, 0, 0)
    full2 = lambda bi, i: (0, 0)
    nl = l // tm
    tok = lambda bi, i: (0, bi * nl + i)

    def fs(a):
        return pl.BlockSpec(a.shape, full2)

    return pl.pallas_call(
        functools.partial(_post_kernel, alpha=alpha),
        grid=(b, nl),
        in_specs=[
            pl.BlockSpec((1, tm, DIFF_WIDTH), row),
            pl.BlockSpec((1, tm, S5_WIDTH), row),
            pl.BlockSpec((1, tm, S5_WIDTH), lambda bi, i: (bi, nl - 1 - i, 0)),
            pl.BlockSpec((1, tm, S5_WIDTH), row),
            pl.BlockSpec((1, tm, RET_WIDTH), row),
            pl.BlockSpec((1, tm, d), row),
            fs(dsk), fs(gw), fs(gb), fs(woa), fs(wob), fs(woc),
            pl.BlockSpec((1, 1, d), vec),
            fs(lng), fs(lnb),
            pl.BlockSpec((1, 1, d), vec),
            pl.BlockSpec((1, 1, d), vec),
            fs(rwt), fs(rb),
        ],
        out_specs=[
            pl.BlockSpec((1, tm, d), row),
            pl.BlockSpec((tm * TOK_TILE, LANES), lambda bi, i: (bi * nl + i, 0)),
            pl.BlockSpec((TOP_K, tm), tok),
            pl.BlockSpec((TOP_K, tm), tok),
        ],
        out_shape=[
            jax.ShapeDtypeStruct((b, l, d), F32),
            jax.ShapeDtypeStruct((b * l * TOK_TILE, LANES), F32),
            jax.ShapeDtypeStruct((TOP_K, b * l), jnp.int32),
            jax.ShapeDtypeStruct((TOP_K, b * l), F32),
        ],
        compiler_params=_cparams("arbitrary", "arbitrary"),
        name="post",
    )(ya, yf, yb, ub, yc, x, dsk, gw, gb, woa, wob, woc, g1, lng, lnb, sc2, sh2, rwt, rb)


TOK_TILE = D_MODEL // LANES


def _tiles_to_rows(ref, lead, first, n):
    return jnp.concatenate(
        [ref[lead, pl.ds(first * TOK_TILE + s, n, stride=TOK_TILE), :] for s in range(TOK_TILE)], axis=1)


def _store_rows_as_tiles(ref, x):
    n = x.shape[0]
    for s in range(TOK_TILE):
        ref[pl.ds(s, n, stride=TOK_TILE), :] = x[:, s * LANES:(s + 1) * LANES]


GATHER_GROUP = 32
GATHER_SLOTS = 3


def _gather_start(src_hbm, row_ref, n, dst, slot, sem, rolled):
    def start(r, k, row):
        src = pl.multiple_of(row_ref[0, 0, r], TOK_TILE)
        pltpu.make_async_copy(src_hbm.at[pl.ds(src, TOK_TILE), :],
                              dst.at[slot, pl.ds(row, TOK_TILE), :], sem.at[slot]).start(priority=k % 2)

    if not rolled:
        for r in range(n):
            start(r, r, r * TOK_TILE)
        return

    def group(g, c):
        for k in range(GATHER_GROUP):
            r = g * GATHER_GROUP + k
            start(r, k, pl.multiple_of(r * TOK_TILE, TOK_TILE))
        return c

    lax.fori_loop(0, n // GATHER_GROUP, group, 0)


def _gather_wait(dst, slot, sem):
    pltpu.make_async_copy(dst.at[(slot + 1) % GATHER_SLOTS], dst.at[slot], sem.at[slot]).wait()


def _gather_pipeline(src_hbm, tables, n, buf, sem, body):
    i = pl.program_id(0)
    slot = i % GATHER_SLOTS
    ahead = (i + 2) % GATHER_SLOTS

    @pl.when(i == 0)
    def _():
        _gather_start(src_hbm, tables[0], n, buf, 0, sem, True)
        _gather_start(src_hbm, tables[1], n, buf, 1, sem, True)

    _gather_wait(buf, slot, sem)
    body(slot, lambda rolled: _gather_start(src_hbm, tables[2], n, buf, ahead, sem, rolled))

    @pl.when(i == pl.num_programs(0) - 1)
    def _():
        _gather_wait(buf, (i + 1) % GATHER_SLOTS, sem)
        _gather_wait(buf, ahead, sem)


def _expert_kernel(be_ref, nu_ref, row0_ref, row1_ref, row2_ref, h_hbm, w1_ref, b1_ref, w2_ref, b2_ref, o_ref,
                   buf, sem):
    eb = row0_ref.shape[2]
    ff = w2_ref.shape[1]
    active = pl.program_id(0) < nu_ref[0]

    def body(slot, start_ahead):
        @pl.when(active)
        def _():
            start_ahead(False)
            hb = _tiles_to_rows(buf, slot, 0, eb).astype(BF16)
            z = jnp.dot(hb, w1_ref[0], preferred_element_type=F32) + b1_ref[0]
            acts = []
            for j in range(ff // LANES):
                z_glu = jnp.minimum(z[:, 2 * j * LANES:(2 * j + 1) * LANES], SWIGLU_LIMIT)
                z_lin = jnp.clip(z[:, (2 * j + 1) * LANES:(2 * j + 2) * LANES], -SWIGLU_LIMIT, SWIGLU_LIMIT)
                acts.append(z_glu * jax.nn.sigmoid(SWIGLU_ALPHA * z_glu) * (z_lin + 1.0))
            act = jnp.concatenate(acts, axis=1).astype(BF16)
            _store_rows_as_tiles(o_ref, jnp.dot(act, w2_ref[0], preferred_element_type=F32) + b2_ref[0])

        @pl.when(jnp.logical_not(active))
        def _():
            start_ahead(True)
            o_ref[...] = jnp.zeros_like(o_ref)

    _gather_pipeline(h_hbm, (row0_ref, row1_ref, row2_ref), eb, buf, sem, body)


def _expert_call(block_e, n_used, slot_row, h2_tiles, w1, b1, w2, b2, layer):
    n_blocks = block_e.shape[0]
    wmap = lambda i, be, nu: (layer * N_EXPERTS + be[i], 0, 0)
    d = w1.shape[1]
    ff = w2.shape[1]
    eb = EXPERT_BLOCK
    rows3 = slot_row.reshape(n_blocks, 1, eb)
    grid_spec = pltpu.PrefetchScalarGridSpec(
        num_scalar_prefetch=2,
        grid=(n_blocks,),
        in_specs=[
            pl.BlockSpec((1, 1, eb), lambda i, be, nu: (i, 0, 0), memory_space=pltpu.SMEM),
            pl.BlockSpec((1, 1, eb), lambda i, be, nu: (jnp.minimum(i + 1, n_blocks - 1), 0, 0),
                         memory_space=pltpu.SMEM),
            pl.BlockSpec((1, 1, eb), lambda i, be, nu: (jnp.minimum(i + 2, n_blocks - 1), 0, 0),
                         memory_space=pltpu.SMEM),
            pl.BlockSpec(memory_space=pl.ANY),
            pl.BlockSpec((1, d, 2 * ff), wmap),
            pl.BlockSpec((1, 1, 2 * ff), wmap),
            pl.BlockSpec((1, ff, d), wmap),
            pl.BlockSpec((1, 1, d), wmap),
        ],
        out_specs=pl.BlockSpec((eb * TOK_TILE, LANES), lambda i, be, nu: (i, 0)),
        scratch_shapes=[pltpu.VMEM((GATHER_SLOTS, eb * TOK_TILE, LANES), F32),
                        pltpu.SemaphoreType.DMA((GATHER_SLOTS,))],
    )
    return pl.pallas_call(
        _expert_kernel,
        grid_spec=grid_spec,
        out_shape=jax.ShapeDtypeStruct((n_blocks * eb * TOK_TILE, LANES), F32),
        compiler_params=_cparams("arbitrary"),
        name="expert",
    )(block_e, n_used, rows3, rows3, rows3, h2_tiles, w1, b1, w2, b2)


def _w1prep_kernel(w_ref, p_ref, o_ref):
    for j in range(w_ref.shape[1] // (2 * LANES)):
        cols = slice(2 * j * LANES, 2 * (j + 1) * LANES)
        o_ref[:, cols] = jnp.dot(w_ref[:, cols].astype(BF16), p_ref[...], preferred_element_type=F32).astype(BF16)


def _w1prep_call(w1_rows, tr):
    n, c = w1_rows.shape
    src = jnp.arange(2 * LANES, dtype=jnp.int32)
    src = jnp.where(src < LANES, 2 * src, 2 * (src - LANES) + 1)
    perm = (jnp.arange(2 * LANES, dtype=jnp.int32)[:, None] == src[None, :]).astype(BF16)
    return pl.pallas_call(
        _w1prep_kernel,
        grid=(n // tr,),
        in_specs=[pl.BlockSpec((tr, c), lambda i: (i, 0)), pl.BlockSpec(perm.shape, lambda i: (0, 0))],
        out_specs=pl.BlockSpec((tr, c), lambda i: (i, 0)),
        out_shape=jax.ShapeDtypeStruct((n, c), BF16),
        compiler_params=_cparams("arbitrary"),
        name="w1prep",
    )(w1_rows, perm)


def _combine_kernel(row0_ref, row1_ref, row2_ref, ys_hbm, gate_ref, x1_ref, g2_ref, lng_ref, lnb_ref, o_ref,
                    buf, sem, *, alpha):
    tm = x1_ref.shape[0]

    def body(slot, start_ahead):
        start_ahead(False)
        gates = gate_ref[...]
        ffo = _tiles_to_rows(buf, slot, 0, tm) * gates[:, 0:1]
        for k in range(1, TOP_K):
            ffo = ffo + _tiles_to_rows(buf, slot, k * tm, tm) * gates[:, k:k + 1]
        o_ref[...] = _ln(alpha * x1_ref[...] + (1.0 + g2_ref[0]) * ffo) * lng_ref[...] + lnb_ref[...]

    _gather_pipeline(ys_hbm, (row0_ref, row1_ref, row2_ref), tm * TOP_K, buf, sem, body)


def _combine_call(dest_row, ys_tiles, gates_t, x1, g2, lng, lnb, alpha, tm):
    b, l, d = x1.shape
    nl = l // tm
    n_tiles = b * nl
    rows3 = dest_row.reshape(n_tiles, tm, TOP_K).transpose(0, 2, 1).reshape(n_tiles, 1, tm * TOP_K)
    full2 = lambda i: (0, 0)
    return pl.pallas_call(
        functools.partial(_combine_kernel, alpha=alpha),
        grid=(n_tiles,),
        in_specs=[
            pl.BlockSpec((1, 1, tm * TOP_K), lambda i: (i, 0, 0), memory_space=pltpu.SMEM),
            pl.BlockSpec((1, 1, tm * TOP_K), lambda i: (jnp.minimum(i + 1, n_tiles - 1), 0, 0),
                         memory_space=pltpu.SMEM),
            pl.BlockSpec((1, 1, tm * TOP_K), lambda i: (jnp.minimum(i + 2, n_tiles - 1), 0, 0),
                         memory_space=pltpu.SMEM),
            pl.BlockSpec(memory_space=pl.ANY),
            pl.BlockSpec((tm, TOP_K), lambda i: (i, 0)),
            pl.BlockSpec((tm, d), lambda i: (i, 0)),
            pl.BlockSpec((1, 1, d), lambda i: (i // nl, 0, 0)),
            pl.BlockSpec(lng.shape, full2),
            pl.BlockSpec(lnb.shape, full2),
        ],
        out_specs=pl.BlockSpec((tm, d), lambda i: (i, 0)),
        out_shape=jax.ShapeDtypeStruct((b * l, d), F32),
        scratch_shapes=[pltpu.VMEM((GATHER_SLOTS, TOP_K * tm * TOK_TILE, LANES), F32),
                        pltpu.SemaphoreType.DMA((GATHER_SLOTS,))],
        compiler_params=_cparams("arbitrary"),
        name="combine",
    )(rows3, rows3, rows3, ys_tiles, gates_t, x1.reshape(b * l, d), g2, lng, lnb).reshape(b, l, d)


def _route(top_idx, n_tok):
    n_assign = n_tok * TOP_K
    n_blocks = n_assign // EXPERT_BLOCK + N_EXPERTS
    e_flat = top_idx.T.reshape(-1)
    order = jnp.argsort(e_flat)
    e_sorted = e_flat[order]
    counts = jnp.sum((e_flat[:, None] == jnp.arange(N_EXPERTS, dtype=jnp.int32)[None, :]).astype(jnp.int32), axis=0)
    start = jnp.cumsum(counts) - counts
    padded = (counts + EXPERT_BLOCK - 1) // EXPERT_BLOCK * EXPERT_BLOCK
    pend = jnp.cumsum(padded)
    pstart = pend - padded
    dest_sorted = pstart[e_sorted] + jnp.arange(n_assign, dtype=jnp.int32) - start[e_sorted]
    _, dest = lax.sort((order.astype(jnp.int32), dest_sorted.astype(jnp.int32)), num_keys=1)
    block_start = jnp.arange(n_blocks, dtype=jnp.int32) * EXPERT_BLOCK
    block_e = jnp.sum((pend[None, :] <= block_start[:, None]).astype(jnp.int32), axis=1)
    block_e = jnp.minimum(block_e, N_EXPERTS - 1)
    slot = jnp.arange(n_blocks * EXPERT_BLOCK, dtype=jnp.int32)
    slot_e = jnp.repeat(block_e, EXPERT_BLOCK)
    within = slot - pstart[slot_e]
    valid = (within < counts[slot_e]) & (slot < pend[-1])
    src = jnp.clip(start[slot_e] + within, 0, n_assign - 1)
    slot_tok = jnp.where(valid, (order[src] // TOP_K).astype(jnp.int32), 0)
    n_used = (pend[-1] // EXPERT_BLOCK).astype(jnp.int32).reshape(1)
    return block_e, n_used, slot_tok, dest


def _pick(n, pref):
    return pref if n % pref == 0 else n


def kernel(x, c, positions, ada_w, ada_b, w_in, w_out, diff_lambda_q1, diff_lambda_k1, diff_lambda_q2,
           diff_lambda_k2, diff_subln_g, s5_lambda_re, s5_lambda_im, s5_log_step, s5_b_re, s5_b_im, s5_c_re,
           s5_c_im, s5_d, s5_glu_w, s5_glu_b, ln_g, ln_b, router_w, router_b, exp_w1, exp_b1, exp_w2, exp_b2):
    bsz, seq, d = x.shape
    depth = ada_w.shape[0]
    n_tok = bsz * seq
    alpha = (2 * depth) ** 0.25

    def tables(inv_freq):
        ang = positions.astype(F32)[..., None] * inv_freq
        cs, sn = jnp.cos(ang), jnp.sin(ang)
        return jnp.concatenate([cs, cs, cs, cs], axis=-1), jnp.concatenate([-sn, sn, -sn, sn], axis=-1)

    cos_a, sin_a = tables(1.0 / (ROPE_THETA ** (jnp.arange(0, DIFF_HEAD_DIM, 2, dtype=F32) / DIFF_HEAD_DIM)))
    cos_r, sin_r = tables(1.0 / (ROPE_THETA ** jnp.linspace(0.0, 1.0, RET_HEAD_DIM // 2, dtype=F32)))
    ret_tables = _ret_tables(RET_CHUNK)

    c_pad = jnp.pad(c.astype(F32), ((0, SUBLANES - bsz), (0, 0)))
    mod = _ada_call(c_pad, ada_w, ada_b)[:, :bsz]

    tm_pre = 512 if seq % 1024 == 0 else seq // 2
    tq = _pick(seq, 1024)
    tc = _pick(seq, 128)
    tm_post = _pick(seq, 512) if seq > 512 else seq // 2
    tm_comb = _pick(seq, 128)

    n_e, two_ff = exp_w1.shape[1], exp_w1.shape[3]
    w1 = _w1prep_call(exp_w1.reshape(depth * n_e * d, two_ff), 512).reshape(depth * n_e, d, two_ff)
    b1 = exp_b1.reshape(depth * n_e, two_ff // (2 * LANES), LANES, 2).transpose(0, 1, 3, 2)
    b1 = b1.reshape(depth * n_e, 1, two_ff).astype(F32)
    w2 = exp_w2.astype(BF16).reshape(depth * n_e, exp_w2.shape[2], d)
    b2 = exp_b2.reshape(depth * n_e, 1, d).astype(F32)

    for l in range(depth):
        sh1, sc1, g1, sh2, sc2, g2 = [mod[l, :, None, i * d:(i + 1) * d] for i in range(6)]
        qa, ka, va, ub, ubr, qr, kr, vr, gr = _pre_call(
            x, sc1, sh1, w_in[l].astype(BF16), cos_a, sin_a, cos_r, sin_r, tm_pre)

        lam_init = 0.8 - 0.6 * math.exp(-0.3 * l)
        ya = _attn_call(qa, ka, va, diff_subln_g[l][None, :].astype(F32),
                        diff_lambda_q1[l][None, :], diff_lambda_k1[l][None, :],
                        diff_lambda_q2[l][None, :], diff_lambda_k2[l][None, :], lam_init, tq)

        bmat_f, bmat_b, cmat, a_rows = _s5_params(
            s5_lambda_re[l], s5_lambda_im[l], s5_log_step[l], s5_b_re[l], s5_b_im[l], s5_c_re[l], s5_c_im[l], bsz)
        y_fwd, y_bwd = _s5_call(ub, ubr, bmat_f, bmat_b, cmat, a_rows, tc)

        yc = _ret_call(qr, kr, vr, gr, ret_tables, RET_CHUNK)

        wo = w_out[l].astype(BF16)
        x1, h2, top_idx, gates = _post_call(
            ya, y_fwd, y_bwd, ub, yc, x,
            s5_d[l][None, :].astype(F32), s5_glu_w[l].astype(BF16), s5_glu_b[l][None, :].astype(F32),
            wo[:DIFF_WIDTH], wo[DIFF_WIDTH:DIFF_WIDTH + S5_WIDTH], wo[DIFF_WIDTH + S5_WIDTH:],
            g1, ln_g[l, 0][None, :], ln_b[l, 0][None, :], sc2, sh2,
            router_w[l].T.astype(F32), router_b[l][:, None].astype(F32), alpha, tm_post)

        block_e, n_used, slot_tok, dest = _route(top_idx, n_tok)
        ys = _expert_call(block_e, n_used, slot_tok * TOK_TILE, h2, w1, b1, w2, b2, l)
        x = _combine_call(dest * TOK_TILE, ys, gates.T, x1, g2, ln_g[l, 1][None, :], ln_b[l, 1][None, :],
                          alpha, tm_comb)
    return x
```

```python
import functools
import math

import jax
import jax.numpy as jnp
from jax import lax
from jax.experimental import pallas as pl
from jax.experimental.pallas import tpu as pltpu

F32 = jnp.float32
BF16 = jnp.bfloat16
HIGHEST = lax.Precision.HIGHEST

LANES = 128
SUBLANES = 8
VMEM_LIMIT = 56 * 1024 * 1024

D_MODEL = 1024
DIFF_HEADS = 4
DIFF_WIDTH = D_MODEL // 2
DIFF_V_DIM = DIFF_WIDTH // DIFF_HEADS
DIFF_HEAD_DIM = DIFF_V_DIM // 2
DIFF_QK = DIFF_HEADS * 2 * DIFF_HEAD_DIM
ROPE_THETA = 10000.0
S5_WIDTH = D_MODEL // 4
S5_GROUP = 16
S5_GROUPS = S5_WIDTH // S5_GROUP
S5_STATE = 64
S5_NS = S5_GROUPS * S5_STATE
RET_HEADS = 4
RET_WIDTH = D_MODEL // 4
RET_HEAD_DIM = RET_WIDTH // RET_HEADS
RET_CHUNK = 128
IN_WIDTH = 2 * DIFF_QK + DIFF_WIDTH + S5_WIDTH + 4 * RET_WIDTH
N_EXPERTS = 32
TOP_K = 4
EXPERT_FF = D_MODEL
SWIGLU_ALPHA = 1.702
SWIGLU_LIMIT = 7.0
EXPERT_BLOCK = 256
LN_EPS = 1e-5
NEG_BIG = -1e30

OFF_QA = 0
OFF_KA = OFF_QA + DIFF_QK
OFF_VA = OFF_KA + DIFF_QK
OFF_UB = OFF_VA + DIFF_WIDTH
OFF_QR = OFF_UB + S5_WIDTH
OFF_KR = OFF_QR + RET_WIDTH
OFF_VR = OFF_KR + RET_WIDTH
OFF_GR = OFF_VR + RET_WIDTH


def _cparams(*sem):
    return pltpu.CompilerParams(dimension_semantics=sem, vmem_limit_bytes=VMEM_LIMIT)


def _ln(x):
    mu = jnp.mean(x, axis=-1, keepdims=True)
    xc = x - mu
    var = jnp.mean(xc * xc, axis=-1, keepdims=True)
    return xc * lax.rsqrt(var + LN_EPS)


def _ada_kernel(c_ref, w_ref, b_ref, o_ref):
    c = c_ref[...]
    cond = c * jax.nn.sigmoid(c)
    o_ref[0] = jnp.dot(cond, w_ref[0], preferred_element_type=F32, precision=HIGHEST) + b_ref[0]


def _ada_call(c_pad, ada_w, ada_b):
    depth, d, n = ada_w.shape
    tn = n // 6
    return pl.pallas_call(
        _ada_kernel,
        grid=(depth, n // tn),
        in_specs=[
            pl.BlockSpec((SUBLANES, d), lambda l, j: (0, 0)),
            pl.BlockSpec((1, d, tn), lambda l, j: (l, 0, j)),
            pl.BlockSpec((1, 1, tn), lambda l, j: (l, 0, j)),
        ],
        out_specs=pl.BlockSpec((1, SUBLANES, tn), lambda l, j: (l, 0, j)),
        out_shape=jax.ShapeDtypeStruct((depth, SUBLANES, n), F32),
        compiler_params=_cparams("arbitrary", "arbitrary"),
        name="ada",
    )(c_pad, ada_w, ada_b.reshape(depth, 1, n))


def _rope_lanes(z, cos, sin_signed, first_half):
    partner = jnp.where(first_half, pltpu.roll(z, LANES - 32, 1), pltpu.roll(z, 32, 1))
    return z * cos + partner * sin_signed


def _anti_identity(n):
    r = lax.broadcasted_iota(jnp.int32, (n, n), 0)
    c = lax.broadcasted_iota(jnp.int32, (n, n), 1)
    return (r + c == n - 1).astype(BF16)


def _pre_kernel(x_ref, sc_ref, sh_ref, w_ref, ca_ref, sa_ref, cr_ref, sr_ref,
                qa_ref, ka_ref, va_ref, ub_ref, ubr_ref, qr_ref, kr_ref, vr_ref, gr_ref):
    tm = x_ref.shape[1]
    h = (_ln(x_ref[0]) * (1.0 + sc_ref[0]) + sh_ref[0]).astype(BF16)
    lane = lax.broadcasted_iota(jnp.int32, (tm, LANES), 1)
    first_half = (lane % 64) < 32

    def proj(off, width):
        return jnp.dot(h, w_ref[:, off:off + width], preferred_element_type=F32)

    def rope_store(z, cos, sin, scale, out_ref):
        for j in range(z.shape[1] // LANES):
            zj = _rope_lanes(z[:, j * LANES:(j + 1) * LANES], cos, sin, first_half)
            out_ref[0, :, j * LANES:(j + 1) * LANES] = (zj * scale).astype(out_ref.dtype)

    ca, sa, cr, sr = ca_ref[0], sa_ref[0], cr_ref[0], sr_ref[0]
    zq = proj(OFF_QA, DIFF_QK)
    for j in range(DIFF_QK // LANES):
        zj = _rope_lanes(zq[:, j * LANES:(j + 1) * LANES], ca, sa, first_half)
        qa_ref[0, j * LANES:(j + 1) * LANES, :] = (zj * (DIFF_HEAD_DIM ** -0.5 * math.log2(math.e))).T.astype(BF16)
    rope_store(proj(OFF_KA, DIFF_QK), ca, sa, 1.0, ka_ref)
    zv = proj(OFF_VA, DIFF_WIDTH)
    for j in range(DIFF_WIDTH // LANES):
        va_ref[0, 0, j * LANES:(j + 1) * LANES, :] = zv[:, j * LANES:(j + 1) * LANES].T.astype(BF16)
    u = proj(OFF_UB, S5_WIDTH)
    ub_ref[0] = u
    ubr_ref[0] = jnp.dot(_anti_identity(tm), u.astype(BF16), preferred_element_type=F32).astype(BF16)
    rope_store(proj(OFF_QR, RET_WIDTH), cr, sr, 1.0, qr_ref)
    rope_store(proj(OFF_KR, RET_WIDTH), cr, sr, RET_HEAD_DIM ** -0.5, kr_ref)
    vr_ref[0] = proj(OFF_VR, RET_WIDTH).astype(vr_ref.dtype)
    gr_ref[0] = proj(OFF_GR, RET_WIDTH).astype(gr_ref.dtype)


def _pre_call(x, sc, sh, w_in_bf, cos_a, sin_a, cos_r, sin_r, tm):
    b, l, d = x.shape
    row = lambda bi, i: (bi, i, 0)
    vec = lambda bi, i: (bi, 0, 0)
    nl = l // tm
    mirrored = lambda bi, i: (bi, nl - 1 - i, 0)
    widths = (DIFF_QK, DIFF_QK, DIFF_WIDTH, S5_WIDTH, S5_WIDTH, RET_WIDTH, RET_WIDTH, RET_WIDTH, RET_WIDTH)
    dtypes = (BF16, BF16, BF16, F32, BF16, BF16, BF16, BF16, BF16)
    maps = (row, row, row, row, mirrored, row, row, row, row)
    specs = [pl.BlockSpec((1, tm, w), m) for w, m in zip(widths, maps)]
    shapes = [jax.ShapeDtypeStruct((b, l, w), dt) for w, dt in zip(widths, dtypes)]
    specs[0] = pl.BlockSpec((1, DIFF_QK, tm), lambda bi, i: (bi, 0, i))
    shapes[0] = jax.ShapeDtypeStruct((b, DIFF_QK, l), BF16)
    specs[2] = pl.BlockSpec((1, 1, DIFF_WIDTH, tm), lambda bi, i: (bi, i, 0, 0))
    shapes[2] = jax.ShapeDtypeStruct((b, nl, DIFF_WIDTH, tm), BF16)
    return pl.pallas_call(
        _pre_kernel,
        grid=(b, l // tm),
        in_specs=[
            pl.BlockSpec((1, tm, d), row),
            pl.BlockSpec((1, 1, d), vec),
            pl.BlockSpec((1, 1, d), vec),
            pl.BlockSpec((d, IN_WIDTH), lambda bi, i: (0, 0)),
            pl.BlockSpec((1, tm, LANES), row),
            pl.BlockSpec((1, tm, LANES), row),
            pl.BlockSpec((1, tm, LANES), row),
            pl.BlockSpec((1, tm, LANES), row),
        ],
        out_specs=specs,
        out_shape=shapes,
        compiler_params=_cparams("arbitrary", "arbitrary"),
        name="pre",
    )(x, sc, sh, w_in_bf, cos_a, sin_a, cos_r, sin_r)


ATTN_CHUNKS_PER_TRIP = 16
ONES_ROWS = 16


def _attn_kernel(qt_ref, k_ref, vt_ref, g_ref, lq1_ref, lk1_ref, lq2_ref, lk2_ref, o_ref, s_a, s_b, *, lam_init):
    tq = qt_ref.shape[2]
    nk, _, tk = vt_ref.shape[1:]
    qt = qt_ref[0]
    row = lax.broadcasted_iota(jnp.int32, qt.shape, 0)
    zero = jnp.zeros_like(qt)
    qq = jnp.concatenate([jnp.where(row < DIFF_HEAD_DIM, qt, zero), jnp.where(row >= DIFF_HEAD_DIM, qt, zero)],
                         axis=1)
    ones = jnp.ones((ONES_ROWS, tk), BF16)

    def scores(j, s_ref):
        start = pl.multiple_of(j * tk, tk)
        s = jnp.dot(k_ref[0, pl.ds(start, tk), :], qq, preferred_element_type=F32)
        s_ref[...] = s
        return jnp.max(s, axis=0, keepdims=True)

    def update(j, s_ref, mc, m, acc):
        m_new = jnp.maximum(m, mc)
        p = jnp.exp2(s_ref[...] - m_new).astype(BF16)
        vt = jnp.concatenate([vt_ref[0, j], ones], axis=0)
        return m_new, jnp.exp2(m - m_new) * acc + jnp.dot(vt, p, preferred_element_type=F32)

    per_trip = math.gcd(nk, ATTN_CHUNKS_PER_TRIP)
    bufs = (s_a, s_b)

    def body(jj, carry):
        mc, m, acc = carry
        j = per_trip * jj
        for c in range(per_trip):
            mc_next = scores(jnp.minimum(j + c + 1, nk - 1), bufs[(c + 1) % 2])
            m, acc = update(j + c, bufs[c % 2], mc, m, acc)
            mc = mc_next
        return mc, m, acc

    m0 = jnp.full((1, 2 * tq), NEG_BIG, F32)
    a0 = jnp.zeros((DIFF_V_DIM + ONES_ROWS, 2 * tq), F32)
    _, _, acc = lax.fori_loop(0, nk // per_trip, body, (scores(0, s_a), m0, a0))
    lam = (jnp.exp(jnp.sum(lq1_ref[...] * lk1_ref[...], axis=1, keepdims=True))
           - jnp.exp(jnp.sum(lq2_ref[...] * lk2_ref[...], axis=1, keepdims=True)) + lam_init)
    on = acc[:DIFF_V_DIM] / acc[DIFF_V_DIM:DIFF_V_DIM + 1]
    ot = on[:, :tq] - lam * on[:, tq:]
    ot = ot * lax.rsqrt(jnp.mean(ot * ot, axis=0, keepdims=True) + LN_EPS)
    o_ref[0] = (ot.T * g_ref[...] * (1.0 - lam_init)).astype(o_ref.dtype)


def _attn_call(qt, ka, vt, subln_g, lq1, lk1, lq2, lk2, lam_init, tq):
    b, l, _ = ka.shape
    nk, tk = vt.shape[1], vt.shape[3]
    cmap = lambda bi, h, i: (0, 0)
    small = pl.BlockSpec((1, DIFF_HEAD_DIM), cmap)
    return pl.pallas_call(
        functools.partial(_attn_kernel, lam_init=lam_init),
        grid=(b, DIFF_HEADS, l // tq),
        in_specs=[
            pl.BlockSpec((1, LANES, tq), lambda bi, h, i: (bi, h, i)),
            pl.BlockSpec((1, l, LANES), lambda bi, h, i: (bi, 0, h)),
            pl.BlockSpec((1, nk, LANES, tk), lambda bi, h, i: (bi, 0, h, 0)),
            pl.BlockSpec((1, DIFF_V_DIM), cmap),
            small, small, small, small,
        ],
        out_specs=pl.BlockSpec((1, tq, LANES), lambda bi, h, i: (bi, i, h)),
        out_shape=jax.ShapeDtypeStruct((b, l, DIFF_WIDTH), BF16),
        scratch_shapes=[pltpu.VMEM((tk, 2 * tq), F32), pltpu.VMEM((tk, 2 * tq), F32)],
        compiler_params=_cparams("arbitrary", "arbitrary", "arbitrary"),
        name="attn",
    )(qt, ka, vt, subln_g, lq1, lk1, lq2, lk2)


def _s5_kernel(uf_ref, ub_ref, bf_ref, bb_ref, c_ref, a_ref, yf_ref, yb_ref, xs_ref, st_ref, *, tc):
    n = S5_NS
    nc = n // LANES
    batch = uf_ref.shape[0]

    @pl.when(pl.program_id(0) == 0)
    def _():
        st_ref[...] = jnp.zeros_like(st_ref)

    for j in range(SUBLANES):
        src = uf_ref[j].astype(BF16) if j < batch else ub_ref[j - batch]
        x = jnp.dot(src, bf_ref[...] if j < batch else bb_ref[...], preferred_element_type=F32)
        for c in range(2 * nc):
            xs_ref[c, pl.ds(j, tc, stride=SUBLANES), :] = x[:, c * LANES:(c + 1) * LANES]

    a_re = a_ref[:, :n]
    a_im = a_ref[:, n:]

    def step(t, carry):
        h_re, h_im = carry
        r0 = pl.multiple_of(t * SUBLANES, SUBLANES)
        x_re = jnp.concatenate([xs_ref[c, pl.ds(r0, SUBLANES), :] for c in range(nc)], axis=1)
        x_im = jnp.concatenate([xs_ref[nc + c, pl.ds(r0, SUBLANES), :] for c in range(nc)], axis=1)
        n_re = a_re * h_re - a_im * h_im + x_re
        n_im = a_re * h_im + a_im * h_re + x_im
        for c in range(nc):
            xs_ref[c, pl.ds(r0, SUBLANES), :] = n_re[:, c * LANES:(c + 1) * LANES]
            xs_ref[nc + c, pl.ds(r0, SUBLANES), :] = n_im[:, c * LANES:(c + 1) * LANES]
        return n_re, n_im

    h_re, h_im = lax.fori_loop(0, tc, step, (st_ref[:, :n], st_ref[:, n:]), unroll=4)
    st_ref[:, :n] = h_re
    st_ref[:, n:] = h_im

    for j in range(SUBLANES):
        d, bi = divmod(j, batch)
        hs = jnp.concatenate([xs_ref[c, pl.ds(j, tc, stride=SUBLANES), :] for c in range(2 * nc)], axis=1)
        yj = jnp.dot(hs.astype(BF16), c_ref[:, d * S5_WIDTH:(d + 1) * S5_WIDTH], preferred_element_type=F32)
        if d == 0:
            yf_ref[bi] = yj
        else:
            yb_ref[bi] = yj


def _s5_call(ub, ubr, bmat_f, bmat_b, cmat, a_rows, tc):
    bsz, l, w = ub.shape
    assert 2 * bsz == SUBLANES
    rows = tc * SUBLANES
    full = lambda i: (0, 0)
    seq_spec = pl.BlockSpec((bsz, tc, w), lambda i: (0, i, 0))
    return pl.pallas_call(
        functools.partial(_s5_kernel, tc=tc),
        grid=(l // tc,),
        in_specs=[
            seq_spec, seq_spec,
            pl.BlockSpec(bmat_f.shape, full),
            pl.BlockSpec(bmat_b.shape, full),
            pl.BlockSpec(cmat.shape, full),
            pl.BlockSpec(a_rows.shape, full),
        ],
        out_specs=[seq_spec, seq_spec],
        out_shape=[jax.ShapeDtypeStruct((bsz, l, w), F32), jax.ShapeDtypeStruct((bsz, l, w), F32)],
        scratch_shapes=[pltpu.VMEM((2 * S5_NS // LANES, rows, LANES), F32), pltpu.VMEM((SUBLANES, 2 * S5_NS), F32)],
        compiler_params=_cparams("arbitrary"),
        name="s5",
    )(ub, ubr, bmat_f, bmat_b, cmat, a_rows)


def _s5_params(lam_re, lam_im, log_step, b_re, b_im, c_re, c_im, batch):
    lr = jnp.minimum(lam_re.astype(F32), -1e-4)
    li = lam_im.astype(F32)
    dt = jnp.exp(log_step.astype(F32))[..., None]
    mag = jnp.exp(lr * dt)
    ab_re = mag * jnp.cos(li * dt)
    ab_im = mag * jnp.sin(li * dt)
    den = lr * lr + li * li
    f_re = ((ab_re - 1.0) * lr + ab_im * li) / den
    f_im = (ab_im * lr - (ab_re - 1.0) * li) / den
    bre = b_re.astype(F32)
    bim = b_im.astype(F32)
    bb_re = f_re[..., None] * bre - f_im[..., None] * bim
    bb_im = f_re[..., None] * bim + f_im[..., None] * bre
    eye = jnp.eye(S5_GROUPS, dtype=F32)

    def in_mat(m):
        return jnp.einsum('gpn,gh->gnhp', m, eye).reshape(S5_WIDTH, S5_NS)

    def out_mat(m):
        return jnp.einsum('gnp,gh->gphn', m, eye).reshape(S5_NS, S5_WIDTH)

    bmats = [jnp.concatenate([in_mat(bb_re[d]), in_mat(bb_im[d])], axis=1).astype(BF16) for d in range(2)]
    cmat = jnp.concatenate(
        [jnp.concatenate([out_mat(c_re[d].astype(F32)), -out_mat(c_im[d].astype(F32))], axis=0) for d in range(2)],
        axis=1).astype(BF16)
    a_dir = jnp.concatenate([ab_re.reshape(2, S5_NS), ab_im.reshape(2, S5_NS)], axis=1)
    a_rows = jnp.repeat(a_dir, batch, axis=0)
    return bmats[0], bmats[1], cmat, a_rows


def _ret_kernel(q_ref, k_ref, v_ref, g_ref, dm_ref, wf_ref, wb_ref, gf_ref, gb_ref, dec_ref,
                o_ref, y_ref, *, chunk):
    seq = q_ref.shape[1]
    n = seq // chunk
    lane = lax.broadcasted_iota(jnp.int32, (chunk, LANES), 1)
    head_lanes = [lane < RET_HEAD_DIM, lane >= RET_HEAD_DIM]
    nt = (((1,), (1,)), ((), ()))
    tn = (((0,), (0,)), ((), ()))

    def load(ref, c0):
        return ref[0, pl.ds(c0, chunk), :]

    def fwd(i, states):
        c0 = pl.multiple_of(i * chunk, chunk)
        qc, kc, vc = load(q_ref, c0), load(k_ref, c0), load(v_ref, c0)
        y = jnp.zeros((chunk, LANES), F32)
        new_states = []
        for h in range(2):
            qm = jnp.where(head_lanes[h], qc, jnp.zeros_like(qc))
            s = lax.dot_general(qm, kc, nt, preferred_element_type=F32) * dm_ref[h]
            yh = jnp.dot(s.astype(BF16), vc, preferred_element_type=F32)
            yh = yh + jnp.dot(qm, states[h].astype(BF16), preferred_element_type=F32) * gf_ref[h]
            kw = (kc.astype(F32) * wf_ref[h]).astype(BF16)
            new_states.append(dec_ref[h, 0:1, :] * states[h]
                              + lax.dot_general(kw, vc, tn, preferred_element_type=F32))
            y = jnp.where(head_lanes[h], yh, y)
        y_ref[pl.ds(c0, chunk), :] = y
        return tuple(new_states)

    z = jnp.zeros((LANES, LANES), F32)
    unroll = math.gcd(n, 4)
    lax.fori_loop(0, n, fwd, (z, z), unroll=unroll)

    def bwd(ii, states):
        i = n - 1 - ii
        c0 = pl.multiple_of(i * chunk, chunk)
        qc, kc, vc = load(q_ref, c0), load(k_ref, c0), load(v_ref, c0)
        y = y_ref[pl.ds(c0, chunk), :]
        new_states = []
        for h in range(2):
            qm = jnp.where(head_lanes[h], qc, jnp.zeros_like(qc))
            yh = jnp.dot(qm, states[h].astype(BF16), preferred_element_type=F32) * gb_ref[h]
            kw = (kc.astype(F32) * wb_ref[h]).astype(BF16)
            new_states.append(dec_ref[h, 1:2, :] * states[h]
                              + lax.dot_general(kw, vc, tn, preferred_element_type=F32))
            y = y + jnp.where(head_lanes[h], yh, 0.0)
        inv = 1.0 / RET_HEAD_DIM
        mu = [jnp.sum(jnp.where(m, y, 0.0), axis=1, keepdims=True) * inv for m in head_lanes]
        yc = y - jnp.where(head_lanes[0], mu[0], mu[1])
        var = [jnp.sum(jnp.where(m, yc * yc, 0.0), axis=1, keepdims=True) * inv for m in head_lanes]
        yn = yc * lax.rsqrt(jnp.where(head_lanes[0], var[0], var[1]) + LN_EPS)
        g = load(g_ref, c0).astype(F32)
        o_ref[0, pl.ds(c0, chunk), :] = (g * jax.nn.sigmoid(g) * yn).astype(o_ref.dtype)
        return tuple(new_states)

    lax.fori_loop(0, n, bwd, (z, z), unroll=unroll)


def _ret_tables(chunk):
    hidx = jnp.arange(RET_HEADS, dtype=F32)
    log_gf = jnp.log1p(-jnp.exp2(-5.0 - hidx))
    log_gb = jnp.log1p(-jnp.exp2(-5.5 - hidx))
    j = jnp.arange(chunk, dtype=F32)
    rel = j[:, None] - j[None, :]
    dmask = jnp.exp(jnp.where(rel >= 0, log_gf[:, None, None], log_gb[:, None, None]) * jnp.abs(rel))
    rows = lambda t: jnp.broadcast_to(t[:, :, None], (RET_HEADS, chunk, LANES))
    wf = rows(jnp.exp(log_gf[:, None] * (chunk - 1.0 - j)))
    wb = rows(jnp.exp(log_gb[:, None] * j))
    gf = rows(jnp.exp(log_gf[:, None] * (j + 1.0)))
    gb = rows(jnp.exp(log_gb[:, None] * (chunk - j)))
    dec = jnp.stack([jnp.exp(log_gf * chunk), jnp.exp(log_gb * chunk)], axis=1)
    dec = jnp.broadcast_to(jnp.pad(dec, ((0, 0), (0, SUBLANES - 2)))[:, :, None], (RET_HEADS, SUBLANES, LANES))
    return dmask, wf, wb, gf, gb, dec


def _ret_call(qr, kr, vr, gr, tables, chunk):
    b, l, _ = qr.shape
    dmask, wf, wb, gf, gb, dec = tables
    seq = lambda bi, p: (bi, 0, p)
    pair = lambda bi, p: (p, 0, 0)
    seq_spec = pl.BlockSpec((1, l, LANES), seq)
    row_spec = pl.BlockSpec((2, chunk, LANES), pair)
    return pl.pallas_call(
        functools.partial(_ret_kernel, chunk=chunk),
        grid=(b, RET_HEADS // 2),
        in_specs=[seq_spec, seq_spec, seq_spec, seq_spec,
                  pl.BlockSpec((2, chunk, chunk), pair),
                  row_spec, row_spec, row_spec, row_spec,
                  pl.BlockSpec((2, SUBLANES, LANES), pair)],
        out_specs=seq_spec,
        out_shape=jax.ShapeDtypeStruct((b, l, RET_WIDTH), BF16),
        scratch_shapes=[pltpu.VMEM((l, LANES), F32)],
        compiler_params=_cparams("arbitrary", "arbitrary"),
        name="ret",
    )(qr, kr, vr, gr, dmask, wf, wb, gf, gb, dec)


POST_ROW_GROUPS = 2


def _post_kernel(ya_ref, yf_ref, yb_ref, ub_ref, yc_ref, x_ref, dsk_ref, gw_ref, gb_ref,
                 woa_ref, wob_ref, woc_ref, g1_ref, lng_ref, lnb_ref, sc2_ref, sh2_ref, rwt_ref, rb_ref,
                 x1_ref, h2_ref, idx_ref, gate_ref, *, alpha):
    tm = x_ref.shape[1]
    hm = tm // POST_ROW_GROUPS
    for grp in range(POST_ROW_GROUPS):
        rows = slice(grp * hm, (grp + 1) * hm)
        ybw = yb_ref[0, tm - (grp + 1) * hm:tm - grp * hm, :]
        hi = ybw.astype(BF16)
        lo = (ybw - hi.astype(F32)).astype(BF16)
        rev = _anti_identity(hm)
        ybw = jnp.dot(rev, hi, preferred_element_type=F32) + jnp.dot(rev, lo, preferred_element_type=F32)
        y = yf_ref[0, rows, :] + ybw + dsk_ref[...] * ub_ref[0, rows, :]
        y = jax.nn.gelu(y).astype(BF16)
        z = jnp.dot(y, gw_ref[...], preferred_element_type=F32) + gb_ref[...]
        yb = (z[:, :S5_WIDTH] * jax.nn.sigmoid(z[:, S5_WIDTH:])).astype(BF16)
        mix = (jnp.dot(ya_ref[0, rows, :], woa_ref[...], preferred_element_type=F32)
               + jnp.dot(yb, wob_ref[...], preferred_element_type=F32)
               + jnp.dot(yc_ref[0, rows, :], woc_ref[...], preferred_element_type=F32))
        x1 = _ln(alpha * x_ref[0, rows, :] + (1.0 + g1_ref[0]) * mix) * lng_ref[...] + lnb_ref[...]
        x1_ref[0, rows, :] = x1
        h2 = _ln(x1) * (1.0 + sc2_ref[0]) + sh2_ref[0]
        for sct in range(TOK_TILE):
            h2_ref[pl.ds(grp * hm * TOK_TILE + sct, hm, stride=TOK_TILE), :] = h2[:, sct * LANES:(sct + 1) * LANES]
        logits = lax.dot_general(rwt_ref[...], h2, (((1,), (1,)), ((), ())),
                                 preferred_element_type=F32, precision=HIGHEST) + rb_ref[...]
        eidx = lax.broadcasted_iota(jnp.int32, logits.shape, 0)
        vals, idxs = [], []
        cur = logits
        for _ in range(TOP_K):
            m = jnp.max(cur, axis=0, keepdims=True)
            i = jnp.min(jnp.where(cur == m, eidx, N_EXPERTS), axis=0, keepdims=True)
            vals.append(m)
            idxs.append(i)
            cur = jnp.where(eidx == i, -jnp.inf, cur)
        tv = jnp.concatenate(vals, axis=0)
        e = jnp.exp(tv - vals[0])
        gate_ref[:, rows] = e / jnp.sum(e, axis=0, keepdims=True)
        idx_ref[:, rows] = jnp.concatenate(idxs, axis=0)


def _post_call(ya, yf, yb, ub, yc, x, dsk, gw, gb, woa, wob, woc, g1, lng, lnb, sc2, sh2, rwt, rb, alpha, tm):
    b, l, d = x.shape
    row = lambda bi, i: (bi, i, 0)
    vec = lambda bi, i: (bi, 0, 0)
    full2 = lambda bi, i: (0, 0)
    nl = l // tm
    tok = lambda bi, i: (0, bi * nl + i)

    def fs(a):
        return pl.BlockSpec(a.shape, full2)

    return pl.pallas_call(
        functools.partial(_post_kernel, alpha=alpha),
        grid=(b, nl),
        in_specs=[
            pl.BlockSpec((1, tm, DIFF_WIDTH), row),
            pl.BlockSpec((1, tm, S5_WIDTH), row),
            pl.BlockSpec((1, tm, S5_WIDTH), lambda bi, i: (bi, nl - 1 - i, 0)),
            pl.BlockSpec((1, tm, S5_WIDTH), row),
            pl.BlockSpec((1, tm, RET_WIDTH), row),
            pl.BlockSpec((1, tm, d), row),
            fs(dsk), fs(gw), fs(gb), fs(woa), fs(wob), fs(woc),
            pl.BlockSpec((1, 1, d), vec),
            fs(lng), fs(lnb),
            pl.BlockSpec((1, 1, d), vec),
            pl.BlockSpec((1, 1, d), vec),
            fs(rwt), fs(rb),
        ],
        out_specs=[
            pl.BlockSpec((1, tm, d), row),
            pl.BlockSpec((tm * TOK_TILE, LANES), lambda bi, i: (bi * nl + i, 0)),
            pl.BlockSpec((TOP_K, tm), tok),
            pl.BlockSpec((TOP_K, tm), tok),
        ],
        out_shape=[
            jax.ShapeDtypeStruct((b, l, d), F32),
            jax.ShapeDtypeStruct((b * l * TOK_TILE, LANES), F32),
            jax.ShapeDtypeStruct((TOP_K, b * l), jnp.int32),
            jax.ShapeDtypeStruct((TOP_K, b * l), F32),
        ],
        compiler_params=_cparams("arbitrary", "arbitrary"),
        name="post",
    )(ya, yf, yb, ub, yc, x, dsk, gw, gb, woa, wob, woc, g1, lng, lnb, sc2, sh2, rwt, rb)


TOK_TILE = D_MODEL // LANES


def _tiles_to_rows(ref, lead, first, n):
    return jnp.concatenate(
        [ref[lead, pl.ds(first * TOK_TILE + s, n, stride=TOK_TILE), :] for s in range(TOK_TILE)], axis=1)


def _store_rows_as_tiles(ref, x):
    n = x.shape[0]
    for s in range(TOK_TILE):
        ref[pl.ds(s, n, stride=TOK_TILE), :] = x[:, s * LANES:(s + 1) * LANES]


GATHER_GROUP = 32
GATHER_SLOTS = 3


def _gather_start(src_hbm, row_ref, n, dst, slot, sem, rolled):
    def start(r, k, row):
        src = pl.multiple_of(row_ref[0, 0, r], TOK_TILE)
        pltpu.make_async_copy(src_hbm.at[pl.ds(src, TOK_TILE), :],
                              dst.at[slot, pl.ds(row, TOK_TILE), :], sem.at[slot]).start(priority=k % 2)

    if not rolled:
        for r in range(n):
            start(r, r, r * TOK_TILE)
        return

    def group(g, c):
        for k in range(GATHER_GROUP):
            r = g * GATHER_GROUP + k
            start(r, k, pl.multiple_of(r * TOK_TILE, TOK_TILE))
        return c

    lax.fori_loop(0, n // GATHER_GROUP, group, 0)


def _gather_wait(dst, slot, sem):
    pltpu.make_async_copy(dst.at[(slot + 1) % GATHER_SLOTS], dst.at[slot], sem.at[slot]).wait()


def _gather_pipeline(src_hbm, tables, n, buf, sem, body):
    i = pl.program_id(0)
    slot = i % GATHER_SLOTS
    ahead = (i + 2) % GATHER_SLOTS

    @pl.when(i == 0)
    def _():
        _gather_start(src_hbm, tables[0], n, buf, 0, sem, True)
        _gather_start(src_hbm, tables[1], n, buf, 1, sem, True)

    _gather_wait(buf, slot, sem)
    body(slot, lambda rolled: _gather_start(src_hbm, tables[2], n, buf, ahead, sem, rolled))

    @pl.when(i == pl.num_programs(0) - 1)
    def _():
        _gather_wait(buf, (i + 1) % GATHER_SLOTS, sem)
        _gather_wait(buf, ahead, sem)


def _expert_kernel(be_ref, nu_ref, row0_ref, row1_ref, row2_ref, h_hbm, w1_ref, b1_ref, w2_ref, b2_ref, o_ref,
                   buf, sem):
    eb = row0_ref.shape[2]
    ff = w2_ref.shape[1]
    active = pl.program_id(0) < nu_ref[0]

    def body(slot, start_ahead):
        @pl.when(active)
        def _():
            start_ahead(False)
            hb = _tiles_to_rows(buf, slot, 0, eb).astype(BF16)
            z = jnp.dot(hb, w1_ref[0], preferred_element_type=F32) + b1_ref[0]
            acts = []
            for j in range(ff // LANES):
                z_glu = jnp.minimum(z[:, 2 * j * LANES:(2 * j + 1) * LANES], SWIGLU_LIMIT)
                z_lin = jnp.clip(z[:, (2 * j + 1) * LANES:(2 * j + 2) * LANES], -SWIGLU_LIMIT, SWIGLU_LIMIT)
                acts.append(z_glu * jax.nn.sigmoid(SWIGLU_ALPHA * z_glu) * (z_lin + 1.0))
            act = jnp.concatenate(acts, axis=1).astype(BF16)
            _store_rows_as_tiles(o_ref, jnp.dot(act, w2_ref[0], preferred_element_type=F32) + b2_ref[0])

        @pl.when(jnp.logical_not(active))
        def _():
            start_ahead(True)
            o_ref[...] = jnp.zeros_like(o_ref)

    _gather_pipeline(h_hbm, (row0_ref, row1_ref, row2_ref), eb, buf, sem, body)


def _expert_call(block_e, n_used, slot_row, h2_tiles, w1, b1, w2, b2, layer):
    n_blocks = block_e.shape[0]
    wmap = lambda i, be, nu: (layer * N_EXPERTS + be[i], 0, 0)
    d = w1.shape[1]
    ff = w2.shape[1]
    eb = EXPERT_BLOCK
    rows3 = slot_row.reshape(n_blocks, 1, eb)
    grid_spec = pltpu.PrefetchScalarGridSpec(
        num_scalar_prefetch=2,
        grid=(n_blocks,),
        in_specs=[
            pl.BlockSpec((1, 1, eb), lambda i, be, nu: (i, 0, 0), memory_space=pltpu.SMEM),
            pl.BlockSpec((1, 1, eb), lambda i, be, nu: (jnp.minimum(i + 1, n_blocks - 1), 0, 0),
                         memory_space=pltpu.SMEM),
            pl.BlockSpec((1, 1, eb), lambda i, be, nu: (jnp.minimum(i + 2, n_blocks - 1), 0, 0),
                         memory_space=pltpu.SMEM),
            pl.BlockSpec(memory_space=pl.ANY),
            pl.BlockSpec((1, d, 2 * ff), wmap),
            pl.BlockSpec((1, 1, 2 * ff), wmap),
            pl.BlockSpec((1, ff, d), wmap),
            pl.BlockSpec((1, 1, d), wmap),
        ],
        out_specs=pl.BlockSpec((eb * TOK_TILE, LANES), lambda i, be, nu: (i, 0)),
        scratch_shapes=[pltpu.VMEM((GATHER_SLOTS, eb * TOK_TILE, LANES), F32),
                        pltpu.SemaphoreType.DMA((GATHER_SLOTS,))],
    )
    return pl.pallas_call(
        _expert_kernel,
        grid_spec=grid_spec,
        out_shape=jax.ShapeDtypeStruct((n_blocks * eb * TOK_TILE, LANES), F32),
        compiler_params=_cparams("arbitrary"),
        name="expert",
    )(block_e, n_used, rows3, rows3, rows3, h2_tiles, w1, b1, w2, b2)


def _w1prep_kernel(w_ref, p_ref, o_ref):
    for j in range(w_ref.shape[1] // (2 * LANES)):
        cols = slice(2 * j * LANES, 2 * (j + 1) * LANES)
        o_ref[:, cols] = jnp.dot(w_ref[:, cols].astype(BF16), p_ref[...], preferred_element_type=F32).astype(BF16)


def _w1prep_call(w1_rows, tr):
    n, c = w1_rows.shape
    src = jnp.arange(2 * LANES, dtype=jnp.int32)
    src = jnp.where(src < LANES, 2 * src, 2 * (src - LANES) + 1)
    perm = (jnp.arange(2 * LANES, dtype=jnp.int32)[:, None] == src[None, :]).astype(BF16)
    return pl.pallas_call(
        _w1prep_kernel,
        grid=(n // tr,),
        in_specs=[pl.BlockSpec((tr, c), lambda i: (i, 0)), pl.BlockSpec(perm.shape, lambda i: (0, 0))],
        out_specs=pl.BlockSpec((tr, c), lambda i: (i, 0)),
        out_shape=jax.ShapeDtypeStruct((n, c), BF16),
        compiler_params=_cparams("arbitrary"),
        name="w1prep",
    )(w1_rows, perm)


def _combine_kernel(row0_ref, row1_ref, row2_ref, ys_hbm, gate_ref, x1_ref, g2_ref, lng_ref, lnb_ref, o_ref,
                    buf, sem, *, alpha):
    tm = x1_ref.shape[0]

    def body(slot, start_ahead):
        start_ahead(False)
        gates = gate_ref[...]
        ffo = _tiles_to_rows(buf, slot, 0, tm) * gates[:, 0:1]
        for k in range(1, TOP_K):
            ffo = ffo + _tiles_to_rows(buf, slot, k * tm, tm) * gates[:, k:k + 1]
        o_ref[...] = _ln(alpha * x1_ref[...] + (1.0 + g2_ref[0]) * ffo) * lng_ref[...] + lnb_ref[...]

    _gather_pipeline(ys_hbm, (row0_ref, row1_ref, row2_ref), tm * TOP_K, buf, sem, body)


def _combine_call(dest_row, ys_tiles, gates_t, x1, g2, lng, lnb, alpha, tm):
    b, l, d = x1.shape
    nl = l // tm
    n_tiles = b * nl
    rows3 = dest_row.reshape(n_tiles, tm, TOP_K).transpose(0, 2, 1).reshape(n_tiles, 1, tm * TOP_K)
    full2 = lambda i: (0, 0)
    return pl.pallas_call(
        functools.partial(_combine_kernel, alpha=alpha),
        grid=(n_tiles,),
        in_specs=[
            pl.BlockSpec((1, 1, tm * TOP_K), lambda i: (i, 0, 0), memory_space=pltpu.SMEM),
            pl.BlockSpec((1, 1, tm * TOP_K), lambda i: (jnp.minimum(i + 1, n_tiles - 1), 0, 0),
                         memory_space=pltpu.SMEM),
            pl.BlockSpec((1, 1, tm * TOP_K), lambda i: (jnp.minimum(i + 2, n_tiles - 1), 0, 0),
                         memory_space=pltpu.SMEM),
            pl.BlockSpec(memory_space=pl.ANY),
            pl.BlockSpec((tm, TOP_K), lambda i: (i, 0)),
            pl.BlockSpec((tm, d), lambda i: (i, 0)),
            pl.BlockSpec((1, 1, d), lambda i: (i // nl, 0, 0)),
            pl.BlockSpec(lng.shape, full2),
            pl.BlockSpec(lnb.shape, full2),
        ],
        out_specs=pl.BlockSpec((tm, d), lambda i: (i, 0)),
        out_shape=jax.ShapeDtypeStruct((b * l, d), F32),
        scratch_shapes=[pltpu.VMEM((GATHER_SLOTS, TOP_K * tm * TOK_TILE, LANES), F32),
                        pltpu.SemaphoreType.DMA((GATHER_SLOTS,))],
        compiler_params=_cparams("arbitrary"),
        name="combine",
    )(rows3, rows3, rows3, ys_tiles, gates_t, x1.reshape(b * l, d), g2, lng, lnb).reshape(b, l, d)


def _route(top_idx, n_tok):
    n_assign = n_tok * TOP_K
    n_blocks = n_assign // EXPERT_BLOCK + N_EXPERTS
    e_flat = top_idx.T.reshape(-1)
    order = jnp.argsort(e_flat)
    e_sorted = e_flat[order]
    counts = jnp.sum((e_flat[:, None] == jnp.arange(N_EXPERTS, dtype=jnp.int32)[None, :]).astype(jnp.int32), axis=0)
    start = jnp.cumsum(counts) - counts
    padded = (counts + EXPERT_BLOCK - 1) // EXPERT_BLOCK * EXPERT_BLOCK
    pend = jnp.cumsum(padded)
    pstart = pend - padded
    dest_sorted = pstart[e_sorted] + jnp.arange(n_assign, dtype=jnp.int32) - start[e_sorted]
    _, dest = lax.sort((order.astype(jnp.int32), dest_sorted.astype(jnp.int32)), num_keys=1)
    block_start = jnp.arange(n_blocks, dtype=jnp.int32) * EXPERT_BLOCK
    block_e = jnp.sum((pend[None, :] <= block_start[:, None]).astype(jnp.int32), axis=1)
    block_e = jnp.minimum(block_e, N_EXPERTS - 1)
    slot = jnp.arange(n_blocks * EXPERT_BLOCK, dtype=jnp.int32)
    slot_e = jnp.repeat(block_e, EXPERT_BLOCK)
    within = slot - pstart[slot_e]
    valid = (within < counts[slot_e]) & (slot < pend[-1])
    src = jnp.clip(start[slot_e] + within, 0, n_assign - 1)
    slot_tok = jnp.where(valid, (order[src] // TOP_K).astype(jnp.int32), 0)
    n_used = (pend[-1] // EXPERT_BLOCK).astype(jnp.int32).reshape(1)
    return block_e, n_used, slot_tok, dest


def _pick(n, pref):
    return pref if n % pref == 0 else n


def kernel(x, c, positions, ada_w, ada_b, w_in, w_out, diff_lambda_q1, diff_lambda_k1, diff_lambda_q2,
           diff_lambda_k2, diff_subln_g, s5_lambda_re, s5_lambda_im, s5_log_step, s5_b_re, s5_b_im, s5_c_re,
           s5_c_im, s5_d, s5_glu_w, s5_glu_b, ln_g, ln_b, router_w, router_b, exp_w1, exp_b1, exp_w2, exp_b2):
    bsz, seq, d = x.shape
    depth = ada_w.shape[0]
    n_tok = bsz * seq
    alpha = (2 * depth) ** 0.25

    def tables(inv_freq):
        ang = positions.astype(F32)[..., None] * inv_freq
        cs, sn = jnp.cos(ang), jnp.sin(ang)
        return jnp.concatenate([cs, cs, cs, cs], axis=-1), jnp.concatenate([-sn, sn, -sn, sn], axis=-1)

    cos_a, sin_a = tables(1.0 / (ROPE_THETA ** (jnp.arange(0, DIFF_HEAD_DIM, 2, dtype=F32) / DIFF_HEAD_DIM)))
    cos_r, sin_r = tables(1.0 / (ROPE_THETA ** jnp.linspace(0.0, 1.0, RET_HEAD_DIM // 2, dtype=F32)))
    ret_tables = _ret_tables(RET_CHUNK)

    c_pad = jnp.pad(c.astype(F32), ((0, SUBLANES - bsz), (0, 0)))
    mod = _ada_call(c_pad, ada_w, ada_b)[:, :bsz]

    tm_pre = 512 if seq % 1024 == 0 else seq // 2
    tq = _pick(seq, 1024)
    tc = _pick(seq, 128)
    tm_post = _pick(seq, 512) if seq > 512 else seq // 2
    tm_comb = _pick(seq, 128)

    n_e, two_ff = exp_w1.shape[1], exp_w1.shape[3]
    w1 = _w1prep_call(exp_w1.reshape(depth * n_e * d, two_ff), 512).reshape(depth * n_e, d, two_ff)
    b1 = exp_b1.reshape(depth * n_e, two_ff // (2 * LANES), LANES, 2).transpose(0, 1, 3, 2)
    b1 = b1.reshape(depth * n_e, 1, two_ff).astype(F32)
    w2 = exp_w2.astype(BF16).reshape(depth * n_e, exp_w2.shape[2], d)
    b2 = exp_b2.reshape(depth * n_e, 1, d).astype(F32)

    for l in range(depth):
        sh1, sc1, g1, sh2, sc2, g2 = [mod[l, :, None, i * d:(i + 1) * d] for i in range(6)]
        qa, ka, va, ub, ubr, qr, kr, vr, gr = _pre_call(
            x, sc1, sh1, w_in[l].astype(BF16), cos_a, sin_a, cos_r, sin_r, tm_pre)

        lam_init = 0.8 - 0.6 * math.exp(-0.3 * l)
        ya = _attn_call(qa, ka, va, diff_subln_g[l][None, :].astype(F32),
                        diff_lambda_q1[l][None, :], diff_lambda_k1[l][None, :],
                        diff_lambda_q2[l][None, :], diff_lambda_k2[l][None, :], lam_init, tq)

        bmat_f, bmat_b, cmat, a_rows = _s5_params(
            s5_lambda_re[l], s5_lambda_im[l], s5_log_step[l], s5_b_re[l], s5_b_im[l], s5_c_re[l], s5_c_im[l], bsz)
        y_fwd, y_bwd = _s5_call(ub, ubr, bmat_f, bmat_b, cmat, a_rows, tc)

        yc = _ret_call(qr, kr, vr, gr, ret_tables, RET_CHUNK)

        wo = w_out[l].astype(BF16)
        x1, h2, top_idx, gates = _post_call(
            ya, y_fwd, y_bwd, ub, yc, x,
            s5_d[l][None, :].astype(F32), s5_glu_w[l].astype(BF16), s5_glu_b[l][None, :].astype(F32),
            wo[:DIFF_WIDTH], wo[DIFF_WIDTH:DIFF_WIDTH + S5_WIDTH], wo[DIFF_WIDTH + S5_WIDTH:],
            g1, ln_g[l, 0][None, :], ln_b[l, 0][None, :], sc2, sh2,
            router_w[l].T.astype(F32), router_b[l][:, None].astype(F32), alpha, tm_post)

        block_e, n_used, slot_tok, dest = _route(top_idx, n_tok)
        ys = _expert_call(block_e, n_used, slot_tok * TOK_TILE, h2, w1, b1, w2, b2, l)
        x = _combine_call(dest * TOK_TILE, ys, gates.T, x1, g2, ln_g[l, 1][None, :], ln_b[l, 1][None, :],
                          alpha, tm_comb)
    return x
```
